```python
import math
import numpy as np
import jax
import jax.numpy as jnp
from jax import lax

D_MODEL = 1024
BATCH = 32
SEQ = 256
DEPTH = 4
DEC_BATCH = 2
DEC_SEQ = 2048
PAST_LEN = 256

GRID_W = 64
HEAD_DIM = 64
N_HEADS_TOTAL = D_MODEL // HEAD_DIM
A_HEADS = 3 * N_HEADS_TOTAL // 8
A_KV_HEADS = A_HEADS // 3
B_HEADS = N_HEADS_TOTAL // 4
C_HEADS = N_HEADS_TOTAL - A_HEADS - B_HEADS
A_WIDTH = A_HEADS * HEAD_DIM
A_KV_WIDTH = A_KV_HEADS * HEAD_DIM
B_WIDTH = B_HEADS * HEAD_DIM
C_WIDTH = C_HEADS * HEAD_DIM
MIX_WIDTH = A_WIDTH + B_WIDTH + C_WIDTH
N_DIR = 2
CONV_K = 3
CHUNK = 64
Q_BLOCK = 128
WIN_R = 8
WIN_C = 16
ROPE_THETA = 10000.0
D_FF = 256 * math.ceil(8 * D_MODEL / (3 * 256))
N_MOD = 6
PROJ_SPLITS = (A_WIDTH, A_KV_WIDTH, A_KV_WIDTH, 3 * B_WIDTH, B_WIDTH,
               N_DIR * B_HEADS, N_DIR * B_HEADS, C_WIDTH, C_WIDTH, C_WIDTH)
IN_WIDTH = sum(PROJ_SPLITS)
EPS = 1e-6

kernel_name = 'hybrid_flow_trunk_step'


def rmsnorm(x, g):
    xf = x.astype(jnp.float32)
    y = xf * lax.rsqrt(jnp.mean(xf * xf, axis=-1, keepdims=True) + EPS)
    return (y * g.astype(jnp.float32)).astype(x.dtype)


def l2norm(x):
    return x * lax.rsqrt(jnp.sum(x * x, axis=-1, keepdims=True) + EPS)


def axial_rope(x):
    s, d = x.shape[1], x.shape[-1]
    quarter = d // 4
    inv = ROPE_THETA ** (-jnp.arange(quarter, dtype=jnp.float32) / quarter)
    t = jnp.arange(s)
    row = (t // GRID_W).astype(jnp.float32)
    col = (t % GRID_W).astype(jnp.float32)
    ang = jnp.concatenate([row[:, None] * inv, col[:, None] * inv], axis=-1)
    cos = jnp.cos(ang)[None, :, None, :]
    sin = jnp.sin(ang)[None, :, None, :]
    xf = x.astype(jnp.float32)
    x1, x2 = xf[..., : d // 2], xf[..., d // 2:]
    return jnp.concatenate([x1 * cos - x2 * sin, x1 * sin + x2 * cos], axis=-1).astype(x.dtype)


def modulation(cvec, w_mod, b_mod):
    m = jax.nn.silu(cvec) @ w_mod + b_mod
    return jnp.split(m[..., None, :], N_MOD, axis=-1)


def split_projection(z):
    idx = np.cumsum(PROJ_SPLITS)[:-1].tolist()
    return jnp.split(z, idx, axis=-1)


def blocked_attention(q, k, v):
    b, s, hq, d = q.shape
    hkv = k.shape[2]
    grp = hq // hkv
    nb = s // Q_BLOCK
    qb = jnp.moveaxis(q.reshape(b, nb, Q_BLOCK, hkv, grp, d), 1, 0)
    scale = d ** -0.5

    def one_block(q_blk):
        sc = jnp.einsum('bqhgd,bkhd->bhgqk', q_blk, k).astype(jnp.float32) * scale
        p = jax.nn.softmax(sc, axis=-1).astype(v.dtype)
        return jnp.einsum('bhgqk,bkhd->bqhgd', p, v)

    o = lax.map(one_block, qb)
    return jnp.moveaxis(o, 0, 1).reshape(b, s, hq * d)


def neighbourhood_attention(q, k, v, k_ctx, v_ctx, rpb):
    b, s, h, d = q.shape
    rows = s // GRID_W
    wr = min(WIN_R, rows)
    r = jnp.arange(rows)
    r_start = jnp.clip(r - wr // 2, 0, rows - wr)
    ridx = r_start[:, None] + jnp.arange(wr)[None, :]
    col = jnp.arange(GRID_W)
    c_start = jnp.clip(col - WIN_C // 2, 0, GRID_W - WIN_C)
    cmask = (col[None, :] >= c_start[:, None]) & (col[None, :] < c_start[:, None] + WIN_C)
    mask = jnp.broadcast_to(cmask[:, None, :], (GRID_W, wr, GRID_W)).reshape(GRID_W, wr * GRID_W)
    qg = q.reshape(b, rows, GRID_W, h, d)
    kg = k.reshape(b, rows, GRID_W, h, d)[:, ridx].reshape(b, rows, wr * GRID_W, h, d)
    vg = v.reshape(b, rows, GRID_W, h, d)[:, ridx].reshape(b, rows, wr * GRID_W, h, d)
    dr = ridx - r[:, None] + (WIN_R - 1)
    dc = jnp.clip(col[None, :] - col[:, None] + (WIN_C - 1), 0, 2 * WIN_C - 2)
    bias = rpb[:, dr[:, None, :, None], dc[None, :, None, :]]
    bias = bias.reshape(h, rows, GRID_W, wr * GRID_W).astype(jnp.float32)
    scale = d ** -0.5
    s_loc = jnp.einsum('brqhd,brkhd->bhrqk', qg, kg).astype(jnp.float32) * scale + bias[None]
    s_loc = jnp.where(mask, s_loc, -jnp.inf)
    s_ctx = jnp.einsum('brqhd,bkhd->bhrqk', qg, k_ctx).astype(jnp.float32) * scale
    p = jax.nn.softmax(jnp.concatenate([s_loc, s_ctx], axis=-1), axis=-1).astype(v.dtype)
    p_loc, p_ctx = p[..., : wr * GRID_W], p[..., wr * GRID_W:]
    o = jnp.einsum('bhrqk,brkhd->brqhd', p_loc, vg) + jnp.einsum('bhrqk,bkhd->brqhd', p_ctx, v_ctx)
    return o.reshape(b, s, h * d)


def gated_delta_chunked(q, k, v, g, beta, s0):
    b, L, h, d = q.shape
    n = L // CHUNK

    def blocks(t):
        t = t.reshape((b, n, CHUNK, h) + t.shape[3:])
        return jnp.moveaxis(jnp.swapaxes(t, 2, 3), 1, 0)

    qc, kc, vc, bc = blocks(q), blocks(k), blocks(v), blocks(beta)
    gc = jnp.cumsum(blocks(g), axis=-1)
    tril = jnp.tril(jnp.ones((CHUNK, CHUNK), dtype=bool))
    strict = jnp.tril(jnp.ones((CHUNK, CHUNK), dtype=bool), -1)
    decay = jnp.exp(jnp.where(tril, gc[..., :, None] - gc[..., None, :], -jnp.inf))
    kb = kc * bc[..., None]
    lmat = jnp.where(strict, jnp.einsum('nbhid,nbhjd->nbhij', kb, kc) * decay, 0.0)
    rhs = jnp.concatenate([vc * bc[..., None], kb * jnp.exp(gc)[..., None]], axis=-1)
    sol = lax.linalg.triangular_solve(lmat + jnp.eye(CHUNK, dtype=lmat.dtype), rhs,
                                      left_side=True, lower=True, unit_diagonal=True)
    u, w = jnp.split(sol, 2, axis=-1)
    a_intra = jnp.where(tril, jnp.einsum('nbhid,nbhjd->nbhij', qc, kc) * decay, 0.0)

    def step(s, xs):
        q_i, k_i, u_i, w_i, g_i, a_i = xs
        v_new = u_i - w_i @ s
        o_i = (q_i * jnp.exp(g_i)[..., None]) @ s + a_i @ v_new
        g_last = g_i[..., -1:]
        s = s * jnp.exp(g_last)[..., None] + jnp.einsum(
            'bhcd,bhce->bhde', k_i * jnp.exp(g_last - g_i)[..., None], v_new)
        return s, o_i

    s_fin, o = lax.scan(step, s0, (qc, kc, u, w, gc, a_intra))
    o = jnp.swapaxes(jnp.moveaxis(o, 0, 1), 2, 3).reshape(b, L, h, d)
    return o, s_fin


def deltanet_prep(b_qkv, b_beta, b_alpha, conv_w, a_log, dt_bias):
    b, L, _ = b_qkv.shape
    pad = CONV_K // 2
    xp = jnp.pad(b_qkv, ((0, 0), (pad, pad), (0, 0)))
    y = xp[:, 0:L] * conv_w[0]
    for j in range(1, CONV_K):
        y = y + xp[:, j:j + L] * conv_w[j]
    y = jax.nn.silu(y).astype(jnp.float32)
    q, k, v = jnp.split(y, 3, axis=-1)
    q = l2norm(q.reshape(b, L, B_HEADS, HEAD_DIM)) * (HEAD_DIM ** -0.5)
    k = l2norm(k.reshape(b, L, B_HEADS, HEAD_DIM))
    v = v.reshape(b, L, B_HEADS, HEAD_DIM)
    beta = jax.nn.sigmoid(b_beta.astype(jnp.float32)).reshape(b, L, N_DIR, B_HEADS)
    g = -jnp.exp(a_log.astype(jnp.float32)) * jax.nn.softplus(
        b_alpha.astype(jnp.float32).reshape(b, L, N_DIR, B_HEADS) + dt_bias.astype(jnp.float32))
    return q, k, v, beta, g


def bidir_delta(q, k, v, beta, g, s0_fwd, s0_bwd):
    o_f, s_f = gated_delta_chunked(q, k, v, g[:, :, 0], beta[:, :, 0], s0_fwd)
    o_b, s_b = gated_delta_chunked(q[:, ::-1], k[:, ::-1], v[:, ::-1],
                                   g[:, ::-1, 1], beta[:, ::-1, 1], s0_bwd)
    return o_f + o_b[:, ::-1], jnp.stack([s_f, s_b], axis=1)


def deltanet_out(o, b_g, g_onorm):
    b, L = o.shape[:2]
    gate = jax.nn.silu(b_g.astype(jnp.float32)).reshape(b, L, B_HEADS, HEAD_DIM)
    return (rmsnorm(o, g_onorm) * gate).reshape(b, L, B_WIDTH)


def merge_groups(o_a, o_b, o_c, g_out_a, g_out_c, w_out):
    y = jnp.concatenate([rmsnorm(o_a, g_out_a), o_b.astype(o_a.dtype), rmsnorm(o_c, g_out_c)], axis=-1)
    return y @ w_out


def context_mixer(h, w_in, g_qk_a, g_out_a, conv_w, a_log, dt_bias, g_onorm_b, g_out_c, w_out):
    b, L, _ = h.shape
    a_q, a_k, a_v, b_qkv, b_g, b_beta, b_alpha, c_q, c_k, c_v = split_projection(h @ w_in)
    qa = rmsnorm(a_q.reshape(b, L, A_HEADS, HEAD_DIM), g_qk_a[0])
    ka = rmsnorm(a_k.reshape(b, L, A_KV_HEADS, HEAD_DIM), g_qk_a[1])
    va = a_v.reshape(b, L, A_KV_HEADS, HEAD_DIM)
    o_a = blocked_attention(qa, ka, va)
    q, k, v, beta, g = deltanet_prep(b_qkv, b_beta, b_alpha, conv_w, a_log, dt_bias)
    zero = jnp.zeros((b, B_HEADS, HEAD_DIM, HEAD_DIM), jnp.float32)
    o_b, s_b = bidir_delta(q, k, v, beta, g, zero, zero)
    o_b = deltanet_out(o_b, b_g, g_onorm_b)
    qc = c_q.reshape(b, L, C_HEADS, HEAD_DIM)
    kc = c_k.reshape(b, L, C_HEADS, HEAD_DIM)
    vc = c_v.reshape(b, L, C_HEADS, HEAD_DIM)
    o_c = blocked_attention(qc, kc, vc)
    m = merge_groups(o_a, o_b, o_c, g_out_a, g_out_c, w_out)
    return m, (ka, va, s_b, kc, vc)


def latent_mixer(h, ka_ctx, va_ctx, s_ctx, kc_ctx, vc_ctx, w_in, g_qk_a, g_out_a, conv_w, a_log,
                 dt_bias, g_onorm_b, rpb, g_out_c, w_out):
    b, L, _ = h.shape
    dt = h.dtype
    a_q, a_k, a_v, b_qkv, b_g, b_beta, b_alpha, c_q, c_k, c_v = split_projection(h @ w_in)
    qa = axial_rope(rmsnorm(a_q.reshape(b, L, A_HEADS, HEAD_DIM), g_qk_a[0]))
    ka = axial_rope(rmsnorm(a_k.reshape(b, L, A_KV_HEADS, HEAD_DIM), g_qk_a[1]))
    va = a_v.reshape(b, L, A_KV_HEADS, HEAD_DIM)
    o_a = blocked_attention(qa, jnp.concatenate([ka, ka_ctx.astype(dt)], axis=1),
                            jnp.concatenate([va, va_ctx.astype(dt)], axis=1))
    q, k, v, beta, g = deltanet_prep(b_qkv, b_beta, b_alpha, conv_w, a_log, dt_bias)
    s_ctx = s_ctx.astype(jnp.float32)
    o_b, _ = bidir_delta(q, k, v, beta, g, s_ctx[:, 0], s_ctx[:, 1])
    o_b = deltanet_out(o_b, b_g, g_onorm_b)
    o_c = neighbourhood_attention(c_q.reshape(b, L, C_HEADS, HEAD_DIM), c_k.reshape(b, L, C_HEADS, HEAD_DIM),
                                  c_v.reshape(b, L, C_HEADS, HEAD_DIM), kc_ctx.astype(dt), vc_ctx.astype(dt), rpb)
    return merge_groups(o_a, o_b, o_c, g_out_a, g_out_c, w_out)


def sandwich_block(x, mod, g_norm, mixer, w_gu, w_down):
    sh1, sc1, gt1, sh2, sc2, gt2 = mod
    h = rmsnorm(x, g_norm[0]) * (1.0 + sc1) + sh1
    m, extra = mixer(h)
    x = x + gt1 * rmsnorm(m, g_norm[1])
    h = rmsnorm(x, g_norm[2]) * (1.0 + sc2) + sh2
    gate, up = jnp.split(h @ w_gu, 2, axis=-1)
    f = (jax.nn.silu(gate) * up) @ w_down
    x = x + gt2 * rmsnorm(f, g_norm[3])
    return x, extra


def setup_inputs(seed: int = 0) -> dict:
    key = jax.random.key(seed)
    ks = jax.random.split(key, 26)
    f32 = jnp.float32
    D = D_MODEL

    def nrm(k, shape, s):
        return s * jax.random.normal(k, shape, f32)

    a_log = jnp.log(jax.random.uniform(ks[20], (DEPTH, N_DIR, B_HEADS), f32, 1.0, 16.0))
    dtv = jnp.exp(jax.random.uniform(ks[21], (DEPTH, N_DIR, B_HEADS), f32, math.log(1e-3), math.log(0.1)))
    dt_bias = dtv + jnp.log(-jnp.expm1(-dtv))
    return {
        'x_prompt': nrm(ks[0], (BATCH, SEQ, D), 1.0),
        'x_sample': nrm(ks[1], (DEC_BATCH, DEC_SEQ, D), 1.0),
        'cache_a_k': nrm(ks[2], (DEC_BATCH, DEPTH, PAST_LEN, A_KV_HEADS, HEAD_DIM), 1.0),
        'cache_a_v': nrm(ks[3], (DEC_BATCH, DEPTH, PAST_LEN, A_KV_HEADS, HEAD_DIM), 1.0),
        'state_b': nrm(ks[4], (DEC_BATCH, DEPTH, N_DIR, B_HEADS, HEAD_DIM, HEAD_DIM), HEAD_DIM ** -0.5),
        'cache_c_k': nrm(ks[5], (DEC_BATCH, DEPTH, PAST_LEN, C_HEADS, HEAD_DIM), 1.0),
        'cache_c_v': nrm(ks[6], (DEC_BATCH, DEPTH, PAST_LEN, C_HEADS, HEAD_DIM), 1.0),
        'c': nrm(ks[7], (DEC_BATCH, D), 1.0),
        'c_ctx': nrm(ks[8], (D,), 1.0),
        'w_mod': nrm(ks[9], (DEPTH, D, N_MOD * D), 0.5 * D ** -0.5),
        'b_mod': nrm(ks[10], (DEPTH, N_MOD * D), 0.01),
        'g_norm': 1.0 + nrm(ks[11], (DEPTH, 4, D), 0.05),
        'w_in': nrm(ks[12], (DEPTH, D, IN_WIDTH), D ** -0.5),
        'g_qk_a': 1.0 + nrm(ks[13], (DEPTH, 2, HEAD_DIM), 0.05),
        'g_out_a': 1.0 + nrm(ks[14], (DEPTH, A_WIDTH), 0.05),
        'conv_w': nrm(ks[15], (DEPTH, CONV_K, 3 * B_WIDTH), CONV_K ** -0.5),
        'a_log': a_log,
        'dt_bias': dt_bias,
        'g_onorm_b': 1.0 + nrm(ks[16], (DEPTH, HEAD_DIM), 0.05),
        'rpb': nrm(ks[17], (DEPTH, C_HEADS, 2 * WIN_R - 1, 2 * WIN_C - 1), 0.1),
        'g_out_c': 1.0 + nrm(ks[18], (DEPTH, C_WIDTH), 0.05),
        'w_out': nrm(ks[19], (DEPTH, MIX_WIDTH, D), MIX_WIDTH ** -0.5),
        'w_gu': nrm(ks[22], (DEPTH, D, 2 * D_FF), D ** -0.5),
        'w_down': nrm(ks[23], (DEPTH, D_FF, D), D_FF ** -0.5),
    }


def reference(x_prompt, x_sample, cache_a_k, cache_a_v, state_b, cache_c_k, cache_c_v, c, c_ctx,
              w_mod, b_mod, g_norm, w_in, g_qk_a, g_out_a, conv_w, a_log, dt_bias, g_onorm_b, rpb,
              g_out_c, w_out, w_gu, w_down):
    xp = x_prompt
    xs = x_sample
    ka_l, va_l, sb_l, kc_l, vc_l = [], [], [], [], []
    for l in range(DEPTH):
        mod_p = modulation(c_ctx, w_mod[l], b_mod[l])
        xp, ctx_t = sandwich_block(
            xp, mod_p, g_norm[l],
            lambda h: context_mixer(h, w_in[l], g_qk_a[l], g_out_a[l], conv_w[l], a_log[l], dt_bias[l],
                                    g_onorm_b[l], g_out_c[l], w_out[l]),
            w_gu[l], w_down[l])
        ka_l.append(ctx_t[0])
        va_l.append(ctx_t[1])
        sb_l.append(ctx_t[2])
        kc_l.append(ctx_t[3])
        vc_l.append(ctx_t[4])
        mod_s = modulation(c, w_mod[l], b_mod[l])
        xs, _ = sandwich_block(
            xs, mod_s, g_norm[l],
            lambda h: (latent_mixer(h, cache_a_k[:, l], cache_a_v[:, l], state_b[:, l], cache_c_k[:, l],
                                    cache_c_v[:, l], w_in[l], g_qk_a[l], g_out_a[l], conv_w[l], a_log[l],
                                    dt_bias[l], g_onorm_b[l], rpb[l], g_out_c[l], w_out[l]), None),
            w_gu[l], w_down[l])
    new_cache_a_k = jnp.stack(ka_l, axis=1)
    new_cache_a_v = jnp.stack(va_l, axis=1)
    new_state_b = jnp.stack(sb_l, axis=1)
    new_cache_c_k = jnp.stack(kc_l, axis=1)
    new_cache_c_v = jnp.stack(vc_l, axis=1)
    return (xp, xs, new_cache_a_k, new_cache_a_v, new_state_b, new_cache_c_k, new_cache_c_v)
```

```python
import functools
import math

import numpy as np
import jax
import jax.numpy as jnp
from jax import lax
from jax.experimental import pallas as pl
from jax.experimental.pallas import tpu as pltpu

F32 = jnp.float32
BF16 = jnp.bfloat16

D_MODEL = 1024
HEAD_DIM = 64
GRID_W = 64
A_HEADS, A_KV_HEADS, B_HEADS, C_HEADS = 6, 2, 4, 6
A_WIDTH, A_KV_WIDTH = A_HEADS * HEAD_DIM, A_KV_HEADS * HEAD_DIM
B_WIDTH, C_WIDTH = B_HEADS * HEAD_DIM, C_HEADS * HEAD_DIM
N_DIR = 2
CHUNK = 64
WIN_R, WIN_C = 8, 16
ROPE_THETA = 10000.0
D_FF = 2816
N_MOD = 6
EPS = 1e-6
LANES = 128
ZA_W = A_WIDTH + 2 * A_KV_WIDTH
ZB_W = 3 * B_WIDTH + B_WIDTH + LANES
ZC_W = 3 * C_WIDTH
GATE_OFF = 4 * B_WIDTH
VMEM_LIMIT = 56 * 1024 * 1024
NBR_G = 8
NBR_KROWS = 16
NEG_INF = float("-inf")


def _cparams(sem):
    return pltpu.CompilerParams(dimension_semantics=sem, vmem_limit_bytes=VMEM_LIMIT)


def _split2(x):
    hi = x.astype(BF16)
    lo = (x - hi.astype(F32)).astype(BF16)
    return hi, lo


def _split3(x):
    hi = x.astype(BF16)
    r1 = x - hi.astype(F32)
    mid = r1.astype(BF16)
    lo = (r1 - mid.astype(F32)).astype(BF16)
    return hi, mid, lo


def _dot(a, b):
    return jnp.dot(a, b, preferred_element_type=F32)


def _mm(a, b):
    return _dot(a.astype(BF16), b.astype(BF16))


def _mm3(a, b):
    ah, al = _split2(a)
    bh, bl = _split2(b)
    return _dot(ah, bh) + (_dot(ah, bl) + _dot(al, bh))


def _nt(a, b):
    return lax.dot_general(a, b, (((1,), (1,)), ((), ())), preferred_element_type=F32)


def _tn(a, b):
    return lax.dot_general(a, b, (((0,), (0,)), ((), ())), preferred_element_type=F32)


def _head_ones(width):
    r = lax.broadcasted_iota(jnp.int32, (width, width), 0) // HEAD_DIM
    c = lax.broadcasted_iota(jnp.int32, (width, width), 1) // HEAD_DIM
    return jnp.where(r == c, 1.0, 0.0).astype(BF16)


def _head_sum(x, width):
    ones = _head_ones(width)
    hi, lo = _split2(x)
    return _dot(hi, ones) + _dot(lo, ones)


def _head_rmsnorm(x, g_row, width):
    ms = _head_sum(x * x, width) * (1.0 / HEAD_DIM)
    return x * lax.rsqrt(ms + EPS) * g_row


def _rmsnorm(x, g_row):
    ms = jnp.mean(x * x, axis=-1, keepdims=True)
    return x * lax.rsqrt(ms + EPS) * g_row


def _silu(x):
    return x * jax.nn.sigmoid(x)


def _rope(x, cos, sin_signed, width):
    reps = width // LANES
    if reps > 1:
        cos = jnp.concatenate([cos] * reps, axis=1)
        sin_signed = jnp.concatenate([sin_signed] * reps, axis=1)
    lane = lax.broadcasted_iota(jnp.int32, x.shape, 1)
    first = (lane & (HEAD_DIM // 2)) == 0
    partner = jnp.where(first, pltpu.roll(x, width - HEAD_DIM // 2, 1), pltpu.roll(x, HEAD_DIM // 2, 1))
    return x * cos + partner * sin_signed


def _mod_kernel(c_ref, w_ref, b_ref, o_ref):
    a = _silu(c_ref[...]).astype(BF16)
    o_ref[0] = _dot(a, w_ref[0].astype(BF16)) + b_ref[0]


def _modulation(cvec8, w_mod, b_mod):
    depth, d, n = w_mod.shape
    tn = 1536
    return pl.pallas_call(
        _mod_kernel,
        out_shape=jax.ShapeDtypeStruct((depth, 8, n), F32),
        grid=(depth, n // tn),
        in_specs=[pl.BlockSpec((8, d), lambda l, j: (0, 0)),
                  pl.BlockSpec((1, d, tn), lambda l, j: (l, 0, j)),
                  pl.BlockSpec((1, 1, tn), lambda l, j: (l, 0, j))],
        out_specs=pl.BlockSpec((1, 8, tn), lambda l, j: (l, 0, j)),
        compiler_params=_cparams(("arbitrary", "arbitrary")),
        name="modulation",
    )(cvec8, w_mod, b_mod.reshape(depth, 1, n))


def _inproj_kernel(x_ref, mod_ref, g_ref, w_ref, za_ref, zb_ref, zc_ref):
    x = x_ref[...]
    h = _rmsnorm(x, g_ref[...]) * (1.0 + mod_ref[0, 1:2, :]) + mod_ref[0, 0:1, :]
    z = _dot(h.astype(BF16), w_ref[...])
    za_ref[...] = z[:, 0:ZA_W]
    zb_ref[...] = z[:, ZA_W:ZA_W + ZB_W]
    zc_ref[...] = z[:, ZA_W + ZB_W:]


def _inproj(x, mod, g_row, w, tm):
    m, d = x.shape
    nb = mod.shape[0]
    per = (m // nb) // tm
    n = w.shape[1]
    return pl.pallas_call(
        _inproj_kernel,
        out_shape=(jax.ShapeDtypeStruct((m, ZA_W), F32),
                   jax.ShapeDtypeStruct((m, ZB_W), F32),
                   jax.ShapeDtypeStruct((m, ZC_W), F32)),
        grid=(m // tm,),
        in_specs=[pl.BlockSpec((tm, d), lambda i: (i, 0)),
                  pl.BlockSpec((1, N_MOD, d), lambda i: (i // per, 0, 0)),
                  pl.BlockSpec((1, d), lambda i: (0, 0)),
                  pl.BlockSpec((d, n), lambda i: (0, 0), pipeline_mode=pl.Buffered(1))],
        out_specs=(pl.BlockSpec((tm, ZA_W), lambda i: (i, 0)),
                   pl.BlockSpec((tm, ZB_W), lambda i: (i, 0)),
                   pl.BlockSpec((tm, ZC_W), lambda i: (i, 0))),
        compiler_params=_cparams(("arbitrary",)),
        name="inproj",
    )(x, mod, g_row, w)


def _attn_kernel(*refs, n_q, n_kv, tq, l_self, l_ctx, norm, rope, emit_k):
    it = iter(refs)
    q_ref, k_ref, v_ref = next(it), next(it), next(it)
    kc_ref = vc_ref = gq_ref = gk_ref = cos_ref = sin_ref = kout_ref = None
    if l_ctx:
        kc_ref, vc_ref = next(it), next(it)
    if norm:
        gq_ref, gk_ref = next(it), next(it)
    if rope:
        cos_ref, sin_ref = next(it), next(it)
    o_ref = next(it)
    if emit_k:
        kout_ref = next(it)
    kbuf, vbuf = next(it), next(it)

    j = pl.program_id(1)
    wq, wk = n_q * HEAD_DIM, n_kv * HEAD_DIM
    grp = n_q // n_kv
    rb = 256

    @pl.when(j == 0)
    def _():
        def body(i, carry):
            r0 = pl.multiple_of(i * rb, rb)
            k = k_ref[pl.ds(r0, rb), :]
            if norm:
                k = _head_rmsnorm(k, gk_ref[...], wk)
            if rope:
                k = _rope(k, cos_ref[pl.ds(r0, rb), :], sin_ref[pl.ds(r0, rb), :], wk)
            if emit_k:
                kout_ref[pl.ds(r0, rb), :] = k
            kbuf[pl.ds(r0, rb), :] = k.astype(BF16)
            vbuf[pl.ds(r0, rb), :] = v_ref[pl.ds(r0, rb), :].astype(BF16)
            return carry
        lax.fori_loop(0, l_self // rb, body, 0)
        if l_ctx:
            kbuf[l_self:l_self + l_ctx, :] = kc_ref[...].astype(BF16)
            vbuf[l_self:l_self + l_ctx, :] = vc_ref[...].astype(BF16)

    q = q_ref[...]
    if norm:
        q = _head_rmsnorm(q, gq_ref[...], wq)
    if rope:
        r0 = pl.multiple_of(j * tq, tq)
        q = _rope(q, cos_ref[pl.ds(r0, tq), :], sin_ref[pl.ds(r0, tq), :], wq)
    q = (q * (HEAD_DIM ** -0.5)).astype(BF16)
    for h in range(n_q):
        kv = h // grp
        qh = q[:, h * HEAD_DIM:(h + 1) * HEAD_DIM]
        s = _nt(qh, kbuf[:, kv * HEAD_DIM:(kv + 1) * HEAD_DIM])
        m = jnp.max(s, axis=-1, keepdims=True)
        p = jnp.exp(s - m)
        l = jnp.sum(p, axis=-1, keepdims=True)
        o = _dot(p.astype(BF16), vbuf[:, kv * HEAD_DIM:(kv + 1) * HEAD_DIM])
        o_ref[:, h * HEAD_DIM:(h + 1) * HEAD_DIM] = o / l


def _attention(z, *, nb, seq, n_q, n_kv, tq, layer=None, kc=None, vc=None, gq=None, gk=None,
               cos=None, sin=None, emit_k=False):
    wq, wk = n_q * HEAD_DIM, n_kv * HEAD_DIM
    nq = seq // tq
    l_ctx = 0 if kc is None else kc.shape[2]
    norm, rope = gq is not None, cos is not None
    kblk, vblk = wq // wk, wq // wk + 1
    args = [z, z, z]
    in_specs = [pl.BlockSpec((tq, wq), lambda b, j: (b * nq + j, 0)),
                pl.BlockSpec((seq, wk), lambda b, j: (b, kblk)),
                pl.BlockSpec((seq, wk), lambda b, j: (b, vblk))]
    if l_ctx:
        args += [kc, vc]
        in_specs += [pl.BlockSpec((None, None, l_ctx, wk), lambda b, j: (b, layer, 0, 0))] * 2
    if norm:
        args += [gq, gk]
        in_specs += [pl.BlockSpec((1, wq), lambda b, j: (0, 0)), pl.BlockSpec((1, wk), lambda b, j: (0, 0))]
    if rope:
        args += [cos, sin]
        in_specs += [pl.BlockSpec((seq, LANES), lambda b, j: (0, 0))] * 2
    out_shape = [jax.ShapeDtypeStruct((nb * seq, wq), F32)]
    out_specs = [pl.BlockSpec((tq, wq), lambda b, j: (b * nq + j, 0))]
    if emit_k:
        out_shape.append(jax.ShapeDtypeStruct((nb * seq, wk), F32))
        out_specs.append(pl.BlockSpec((seq, wk), lambda b, j: (b, 0)))
    kern = functools.partial(_attn_kernel, n_q=n_q, n_kv=n_kv, tq=tq, l_self=seq, l_ctx=l_ctx,
                             norm=norm, rope=rope, emit_k=emit_k)
    return pl.pallas_call(
        kern,
        out_shape=tuple(out_shape),
        grid=(nb, nq),
        in_specs=in_specs,
        out_specs=tuple(out_specs),
        scratch_shapes=[pltpu.VMEM((seq + l_ctx, wk), BF16), pltpu.VMEM((seq + l_ctx, wk), BF16)],
        compiler_params=_cparams(("arbitrary", "arbitrary")),
        name="attention",
    )(*args)


def _nbr_plan(rows):
    nblk = rows // NBR_G
    kstart, pairs, ent = [], {}, []
    for blk in range(nblk):
        r0 = blk * NBR_G
        ks = min(max(r0 - WIN_R // 2, 0), rows - NBR_KROWS)
        kstart.append(ks)
        for i in range(NBR_G):
            r = r0 + i
            rs = min(max(r - WIN_R // 2, 0), rows - WIN_R)
            assert ks <= rs and rs + WIN_R <= ks + NBR_KROWS
            for p in range(NBR_KROWS // 2):
                ab = []
                for krow in (ks + 2 * p, ks + 2 * p + 1):
                    valid = rs <= krow < rs + WIN_R
                    ab.append(krow - r + WIN_R if valid else 0)
                ent.append(pairs.setdefault(tuple(ab), len(pairs)))
    pair_list = sorted(pairs, key=pairs.get)
    return (np.asarray(kstart, np.int32), np.asarray(ent, np.int32),
            np.asarray([a for a, _ in pair_list], np.int32), np.asarray([b for _, b in pair_list], np.int32))


def _nbr_bias_tables(rpb, rows):
    _, _, left, right = _nbr_plan(rows)
    col = np.arange(GRID_W)
    c_start = np.clip(col - WIN_C // 2, 0, GRID_W - WIN_C)
    cmask = (col[None, :] >= c_start[:, None]) & (col[None, :] < c_start[:, None] + WIN_C)
    dc = np.clip(col[None, :] - col[:, None] + (WIN_C - 1), 0, 2 * WIN_C - 2)
    t = jnp.take(rpb.astype(F32), jnp.asarray(dc.reshape(-1)), axis=-1)
    t = t.reshape(rpb.shape[:3] + (GRID_W, GRID_W))
    t = jnp.where(jnp.asarray(cmask), t, NEG_INF)
    masked = jnp.full(t.shape[:2] + (1, GRID_W, GRID_W), NEG_INF, F32)
    tpad = jnp.concatenate([masked, t], axis=2)
    return jnp.concatenate([jnp.take(tpad, jnp.asarray(left), axis=2),
                            jnp.take(tpad, jnp.asarray(right), axis=2)], axis=-1)


def _nbr_kernel(ks_ref, ent_ref, q_ref, k_ref, v_ref, kc_ref, vc_ref, tt_ref, o_ref,
                kbuf, vbuf, kcbuf, vcbuf, s_ref, *, seq):
    blk = pl.program_id(1)
    tq = NBR_G * GRID_W
    nk = NBR_KROWS * GRID_W
    npair = NBR_KROWS // 2
    rb = 256

    @pl.when(blk == 0)
    def _():
        def body(i, carry):
            r0 = pl.multiple_of(i * rb, rb)
            kbuf[pl.ds(r0, rb), :] = k_ref[pl.ds(r0, rb), :].astype(BF16)
            vbuf[pl.ds(r0, rb), :] = v_ref[pl.ds(r0, rb), :].astype(BF16)
            return carry
        lax.fori_loop(0, seq // rb, body, 0)
        kcbuf[...] = kc_ref[...].astype(BF16)
        vcbuf[...] = vc_ref[...].astype(BF16)

    ks = pl.multiple_of(ks_ref[blk] * GRID_W, GRID_W)
    q = (q_ref[...] * (HEAD_DIM ** -0.5)).astype(BF16)
    for h in range(C_HEADS):
        hs = slice(h * HEAD_DIM, (h + 1) * HEAD_DIM)
        qh = q[:, hs]
        s_ref[...] = _nt(qh, kbuf[pl.ds(ks, nk), hs])
        for i in range(NBR_G):
            for p in range(npair):
                e = ent_ref[(blk * NBR_G + i) * npair + p]
                rs_, cs_ = slice(i * GRID_W, (i + 1) * GRID_W), slice(p * LANES, (p + 1) * LANES)
                s_ref[rs_, cs_] = s_ref[rs_, cs_] + tt_ref[h, e]
        s_loc = s_ref[...]
        s_ctx = _nt(qh, kcbuf[:, hs])
        m = jnp.maximum(jnp.max(s_loc, axis=-1, keepdims=True), jnp.max(s_ctx, axis=-1, keepdims=True))
        p_loc = jnp.exp(s_loc - m)
        p_ctx = jnp.exp(s_ctx - m)
        l = jnp.sum(p_loc, axis=-1, keepdims=True) + jnp.sum(p_ctx, axis=-1, keepdims=True)
        o = _dot(p_loc.astype(BF16), vbuf[pl.ds(ks, nk), hs]) + _dot(p_ctx.astype(BF16), vcbuf[:, hs])
        o_ref[:, hs] = o / l


def _nbr_attention(zc, kc, vc, tt, *, nb, seq, layer):
    rows = seq // GRID_W
    kstart, ent, _, _ = _nbr_plan(rows)
    nblk = rows // NBR_G
    tq = NBR_G * GRID_W
    nk = NBR_KROWS * GRID_W
    l_ctx = kc.shape[2]
    n_ent = tt.shape[2]
    smem = pl.BlockSpec(memory_space=pltpu.SMEM)
    return pl.pallas_call(
        functools.partial(_nbr_kernel, seq=seq),
        out_shape=jax.ShapeDtypeStruct((nb * seq, C_WIDTH), F32),
        grid=(nb, nblk),
        in_specs=[smem, smem,
                  pl.BlockSpec((tq, C_WIDTH), lambda b, j: (b * nblk + j, 0)),
                  pl.BlockSpec((seq, C_WIDTH), lambda b, j: (b, 1)),
                  pl.BlockSpec((seq, C_WIDTH), lambda b, j: (b, 2)),
                  pl.BlockSpec((None, None, l_ctx, C_WIDTH), lambda b, j: (b, layer, 0, 0)),
                  pl.BlockSpec((None, None, l_ctx, C_WIDTH), lambda b, j: (b, layer, 0, 0)),
                  pl.BlockSpec((None, C_HEADS, n_ent, GRID_W, LANES), lambda b, j: (layer, 0, 0, 0, 0))],
        out_specs=pl.BlockSpec((tq, C_WIDTH), lambda b, j: (b * nblk + j, 0)),
        scratch_shapes=[pltpu.VMEM((seq, C_WIDTH), BF16), pltpu.VMEM((seq, C_WIDTH), BF16),
                        pltpu.VMEM((l_ctx, C_WIDTH), BF16), pltpu.VMEM((l_ctx, C_WIDTH), BF16),
                        pltpu.VMEM((tq, nk), F32)],
        compiler_params=_cparams(("arbitrary", "arbitrary")),
        name="nbr_attention",
    )(jnp.asarray(kstart), jnp.asarray(ent), zc, zc, zc, kc, vc, tt)


def _unit_tri_inverse(lmat, lower):
    n = lmat.shape[0]
    ri = lax.broadcasted_iota(jnp.int32, (n, n), 0)
    ci = lax.broadcasted_iota(jnp.int32, (n, n), 1)
    t = jnp.where(ri == ci, 1.0, 0.0).astype(F32)
    s = 1
    while s < n:
        same_pair = (ri // (2 * s)) == (ci // (2 * s))
        r_hi, c_hi = (ri // s) % 2 == 1, (ci // s) % 2 == 1
        sel = same_pair & ((r_hi & ~c_hi) if lower else (~r_hi & c_hi))
        off = jnp.where(sel, lmat, 0.0)
        t = t - off if s == 1 else t - _mm3(_mm3(t, off), t)
        s *= 2
    return t


def _delta_kernel(*refs, seq, has_s0):
    it = iter(refs)
    zb_ref = next(it)
    s0_ref = next(it) if has_s0 else None
    convw_ref, alog_ref, dtb_ref, gon_ref = next(it), next(it), next(it), next(it)
    o_ref, sfin_ref = next(it), next(it)
    q_s, k_s, v_s, gc_s, beta_s, gct_s, of_s, ob_s, st_s = (next(it) for _ in range(9))

    n = seq // CHUNK
    c64 = CHUNK
    row = lax.broadcasted_iota(jnp.int32, (c64, 1), 0)
    lane = lax.broadcasted_iota(jnp.int32, (c64, LANES), 1)
    w0, w1, w2 = convw_ref[0:1, :], convw_ref[1:2, :], convw_ref[2:3, :]
    neg_decay_rate = -jnp.exp(alog_ref[...])
    eye8 = jnp.where(lax.broadcasted_iota(jnp.int32, (8, LANES), 0) == lax.broadcasted_iota(jnp.int32, (8, LANES), 1),
                     1.0, 0.0).astype(BF16)

    def prep(c, carry):
        r0 = pl.multiple_of(c * c64, c64)
        x = zb_ref[pl.ds(r0, c64), 0:3 * B_WIDTH]
        prev = zb_ref[pl.ds(jnp.maximum(r0 - 1, 0), 1), 0:3 * B_WIDTH] * (c > 0).astype(F32)
        nxt = zb_ref[pl.ds(jnp.minimum(r0 + c64, seq - 1), 1), 0:3 * B_WIDTH] * (c < n - 1).astype(F32)
        x_up = jnp.where(row == 0, prev, pltpu.roll(x, 1, 0))
        x_dn = jnp.where(row == c64 - 1, nxt, pltpu.roll(x, c64 - 1, 0))
        y = _silu(x_up * w0 + x * w1 + x_dn * w2)
        q, k, v = y[:, 0:B_WIDTH], y[:, B_WIDTH:2 * B_WIDTH], y[:, 2 * B_WIDTH:]
        q_s[pl.ds(r0, c64), :] = q * lax.rsqrt(_head_sum(q * q, B_WIDTH) + EPS) * (HEAD_DIM ** -0.5)
        k_s[pl.ds(r0, c64), :] = k * lax.rsqrt(_head_sum(k * k, B_WIDTH) + EPS)
        v_s[pl.ds(r0, c64), :] = v
        ga = zb_ref[pl.ds(r0, c64), GATE_OFF:GATE_OFF + LANES]
        beta_s[pl.ds(r0, c64), :] = jax.nn.sigmoid(ga)
        a = pltpu.roll(ga, LANES - N_DIR * B_HEADS, 1) + dtb_ref[...]
        g = neg_decay_rate * (jnp.maximum(a, 0.0) + jnp.log(1.0 + jnp.exp(-jnp.abs(a))))
        cf, cr = g, g
        sh = 1
        while sh < c64:
            cf = cf + jnp.where(row >= sh, pltpu.roll(cf, sh, 0), 0.0)
            cr = cr + jnp.where(row < c64 - sh, pltpu.roll(cr, c64 - sh, 0), 0.0)
            sh *= 2
        gc = jnp.where(lane < B_HEADS, cf, cr)
        gc_s[pl.ds(r0, c64), :] = gc
        h3 = _split3(gc)
        gct_s[c] = _nt(eye8, h3[0]) + (_nt(eye8, h3[1]) + _nt(eye8, h3[2]))
        return carry

    lax.fori_loop(0, n, prep, 0)

    for d in range(N_DIR):
        for h in range(B_HEADS):
            st_s[d, h] = s0_ref[d, h] if has_s0 else jnp.zeros((HEAD_DIM, HEAD_DIM), F32)

    ri = lax.broadcasted_iota(jnp.int32, (c64, c64), 0)
    ci = lax.broadcasted_iota(jnp.int32, (c64, c64), 1)

    def scan(t, carry):
        for d in range(N_DIR):
            c = t if d == 0 else n - 1 - t
            r0 = pl.multiple_of(c * c64, c64)
            incl = (ri >= ci) if d == 0 else (ri <= ci)
            strict = (ri > ci) if d == 0 else (ri < ci)
            last = c64 - 1 if d == 0 else 0
            qc, kc, vc = q_s[pl.ds(r0, c64), :], k_s[pl.ds(r0, c64), :], v_s[pl.ds(r0, c64), :]
            gcc, btc, gct = gc_s[pl.ds(r0, c64), :], beta_s[pl.ds(r0, c64), :], gct_s[c]
            out_s = of_s if d == 0 else ob_s
            for h in range(B_HEADS):
                col = d * B_HEADS + h
                hs = slice(h * HEAD_DIM, (h + 1) * HEAD_DIM)
                g_col, g_row, b_col = gcc[:, col:col + 1], gct[col:col + 1, :], btc[:, col:col + 1]
                g_last = gcc[last:last + 1, col:col + 1]
                qh, kh, vh = qc[:, hs], kc[:, hs], vc[:, hs]
                decay = jnp.exp(jnp.where(incl, g_col - g_row, NEG_INF))
                kb = kh * b_col
                lmat = jnp.where(strict, _nt(kb.astype(BF16), kh.astype(BF16)) * decay, 0.0)
                tinv = _unit_tri_inverse(lmat, lower=(d == 0))
                eg = jnp.exp(g_col)
                u = _mm3(tinv, vh * b_col)
                w = _mm3(tinv, kb * eg)
                a_intra = jnp.where(incl, _nt(qh.astype(BF16), kh.astype(BF16)) * decay, 0.0)
                s = st_s[d, h]
                v_new = u - _mm(w, s)
                out_s[pl.ds(r0, c64), hs] = _mm(qh * eg, s) + _mm(a_intra, v_new)
                k_dec = kh * jnp.exp(g_last - g_col)
                st_s[d, h] = s * jnp.exp(g_last) + _tn(k_dec.astype(BF16), v_new.astype(BF16))
        return carry

    lax.fori_loop(0, n, scan, 0)

    def finish(c, carry):
        r0 = pl.multiple_of(c * c64, c64)
        o = of_s[pl.ds(r0, c64), :] + ob_s[pl.ds(r0, c64), :]
        y = _head_rmsnorm(o, gon_ref[...], B_WIDTH)
        o_ref[pl.ds(r0, c64), :] = y * _silu(zb_ref[pl.ds(r0, c64), 3 * B_WIDTH:4 * B_WIDTH])
        return carry

    lax.fori_loop(0, n, finish, 0)
    for d in range(N_DIR):
        for h in range(B_HEADS):
            sfin_ref[d, h] = st_s[d, h]


def _deltanet(zb, s0, conv_w, a_log, dt_bias, g_onorm, *, nb, seq, layer=None):
    n = seq // CHUNK
    has_s0 = s0 is not None
    pad = LANES - N_DIR * B_HEADS
    alog_row = jnp.pad(a_log.reshape(1, -1).astype(F32), ((0, 0), (0, pad)))
    dtb_row = jnp.pad(dt_bias.reshape(1, -1).astype(F32), ((0, 0), (0, pad)))
    gon_row = jnp.tile(g_onorm.astype(F32), B_HEADS).reshape(1, B_WIDTH)
    st_shape = (N_DIR, B_HEADS, HEAD_DIM, HEAD_DIM)
    args = [zb]
    in_specs = [pl.BlockSpec((seq, ZB_W), lambda b: (b, 0))]
    if has_s0:
        args.append(s0)
        in_specs.append(pl.BlockSpec((None, None) + st_shape, lambda b: (b, layer, 0, 0, 0, 0)))
    args += [conv_w.astype(F32), alog_row, dtb_row, gon_row]
    in_specs += [pl.BlockSpec((3, 3 * B_WIDTH), lambda b: (0, 0)),
                 pl.BlockSpec((1, LANES), lambda b: (0, 0)),
                 pl.BlockSpec((1, LANES), lambda b: (0, 0)),
                 pl.BlockSpec((1, B_WIDTH), lambda b: (0, 0))]
    return pl.pallas_call(
        functools.partial(_delta_kernel, seq=seq, has_s0=has_s0),
        out_shape=(jax.ShapeDtypeStruct((nb * seq, B_WIDTH), F32),
                   jax.ShapeDtypeStruct((nb,) + st_shape, F32)),
        grid=(nb,),
        in_specs=in_specs,
        out_specs=(pl.BlockSpec((seq, B_WIDTH), lambda b: (b, 0)),
                   pl.BlockSpec((None,) + st_shape, lambda b: (b, 0, 0, 0, 0))),
        scratch_shapes=[pltpu.VMEM((seq, B_WIDTH), F32), pltpu.VMEM((seq, B_WIDTH), F32),
                        pltpu.VMEM((seq, B_WIDTH), F32), pltpu.VMEM((seq, LANES), F32),
                        pltpu.VMEM((seq, LANES), F32), pltpu.VMEM((n, 8, CHUNK), F32),
                        pltpu.VMEM((seq, B_WIDTH), F32), pltpu.VMEM((seq, B_WIDTH), F32),
                        pltpu.VMEM(st_shape, F32)],
        compiler_params=_cparams(("arbitrary",)),
        name="deltanet",
    )(*args)


def _post_kernel(x_ref, oa_ref, ob_ref, oc_ref, mod_ref, gn_ref, ga_ref, gc_ref,
                 wo_ref, wgu_ref, wd_ref, y_ref):
    x = x_ref[...]
    ya = _rmsnorm(oa_ref[...], ga_ref[...])
    yc = _rmsnorm(oc_ref[...], gc_ref[...])
    y = jnp.concatenate([ya, ob_ref[...], yc], axis=-1).astype(BF16)
    m = _dot(y, wo_ref[...])
    x1 = x + mod_ref[0, 2:3, :] * _rmsnorm(m, gn_ref[1:2, :])
    h = _rmsnorm(x1, gn_ref[2:3, :]) * (1.0 + mod_ref[0, 4:5, :]) + mod_ref[0, 3:4, :]
    gu = _dot(h.astype(BF16), wgu_ref[...])
    act = (_silu(gu[:, 0:D_FF]) * gu[:, D_FF:]).astype(BF16)
    f = _dot(act, wd_ref[...])
    y_ref[...] = x1 + mod_ref[0, 5:6, :] * _rmsnorm(f, gn_ref[3:4, :])


def _post(x, oa, ob, oc, mod, g_norm, g_out_a, g_out_c, w_out, w_gu, w_down, tm):
    m, d = x.shape
    nb = mod.shape[0]
    per = (m // nb) // tm
    row = lambda i: (i, 0)
    const = lambda i: (0, 0)
    single = pl.Buffered(1)
    return pl.pallas_call(
        _post_kernel,
        out_shape=jax.ShapeDtypeStruct((m, d), F32),
        grid=(m // tm,),
        in_specs=[pl.BlockSpec((tm, d), row),
                  pl.BlockSpec((tm, A_WIDTH), row),
                  pl.BlockSpec((tm, B_WIDTH), row),
                  pl.BlockSpec((tm, C_WIDTH), row),
                  pl.BlockSpec((1, N_MOD, d), lambda i: (i // per, 0, 0)),
                  pl.BlockSpec((4, d), const),
                  pl.BlockSpec((1, A_WIDTH), const),
                  pl.BlockSpec((1, C_WIDTH), const),
                  pl.BlockSpec(w_out.shape, const, pipeline_mode=single),
                  pl.BlockSpec(w_gu.shape, const, pipeline_mode=single),
                  pl.BlockSpec(w_down.shape, const, pipeline_mode=single)],
        out_specs=pl.BlockSpec((tm, d), row),
        compiler_params=_cparams(("arbitrary",)),
        name="post",
    )(x, oa, ob, oc, mod, g_norm, g_out_a, g_out_c, w_out, w_gu, w_down)


def _rope_tables(seq):
    quarter = HEAD_DIM // 4
    inv = ROPE_THETA ** (-jnp.arange(quarter, dtype=F32) / quarter)
    t = jnp.arange(seq)
    rowp = (t // GRID_W).astype(F32)
    colp = (t % GRID_W).astype(F32)
    ang = jnp.concatenate([rowp[:, None] * inv, colp[:, None] * inv], axis=-1)
    cos, sin = jnp.cos(ang), jnp.sin(ang)
    reps = LANES // HEAD_DIM
    return (jnp.tile(jnp.concatenate([cos, cos], axis=-1), (1, reps)),
            jnp.tile(jnp.concatenate([-sin, sin], axis=-1), (1, reps)))


def _permute_w_in(w_in):
    cut = ZA_W + 4 * B_WIDTH + 2 * N_DIR * B_HEADS
    pad = jnp.zeros(w_in.shape[:-1] + (LANES - 2 * N_DIR * B_HEADS,), w_in.dtype)
    return jnp.concatenate([w_in[..., :cut], pad, w_in[..., cut:]], axis=-1).astype(BF16)


def kernel(x_prompt, x_sample, cache_a_k, cache_a_v, state_b, cache_c_k, cache_c_v, c, c_ctx, w_mod, b_mod,
           g_norm, w_in, g_qk_a, g_out_a, conv_w, a_log, dt_bias, g_onorm_b, rpb, g_out_c, w_out, w_gu, w_down):
    nbp, seq_p, d = x_prompt.shape
    nbs, seq_s, _ = x_sample.shape
    depth = w_mod.shape[0]
    past = cache_a_k.shape[2]

    cvec = jnp.concatenate([c_ctx[None, :], c, jnp.zeros((8 - 1 - nbs, d), F32)], axis=0)
    mods = _modulation(cvec, w_mod, b_mod)
    w_in_p = _permute_w_in(w_in)
    w_out_b, w_gu_b, w_down_b = w_out.astype(BF16), w_gu.astype(BF16), w_down.astype(BF16)
    cos, sin = _rope_tables(seq_s)
    tt = _nbr_bias_tables(rpb, seq_s // GRID_W)
    ck_a = cache_a_k.reshape(nbs, depth, past, A_KV_WIDTH)
    cv_a = cache_a_v.reshape(nbs, depth, past, A_KV_WIDTH)
    ck_c = cache_c_k.reshape(nbs, depth, past, C_WIDTH)
    cv_c = cache_c_v.reshape(nbs, depth, past, C_WIDTH)

    xp = x_prompt.reshape(nbp * seq_p, d)
    xs = x_sample.reshape(nbs * seq_s, d)
    ka_l, va_l, sb_l, kc_l, vc_l = [], [], [], [], []
    for l in range(depth):
        gq = jnp.tile(g_qk_a[l, 0], A_HEADS).reshape(1, A_WIDTH)
        gk = jnp.tile(g_qk_a[l, 1], A_KV_HEADS).reshape(1, A_KV_WIDTH)
        goa, goc = g_out_a[l].reshape(1, A_WIDTH), g_out_c[l].reshape(1, C_WIDTH)
        g0 = g_norm[l, 0:1]
        mod_p = mods[l, 0:1].reshape(1, N_MOD, d)
        mod_s = mods[l, 1:1 + nbs].reshape(nbs, N_MOD, d)

        za, zb, zc = _inproj(xp, mod_p, g0, w_in_p[l], tm=256)
        o_a, ka = _attention(za, nb=nbp, seq=seq_p, n_q=A_HEADS, n_kv=A_KV_HEADS, tq=seq_p,
                             gq=gq, gk=gk, emit_k=True)
        o_b, s_b = _deltanet(zb, None, conv_w[l], a_log[l], dt_bias[l], g_onorm_b[l], nb=nbp, seq=seq_p)
        (o_c,) = _attention(zc, nb=nbp, seq=seq_p, n_q=C_HEADS, n_kv=C_HEADS, tq=seq_p)
        xp = _post(xp, o_a, o_b, o_c, mod_p, g_norm[l], goa, goc, w_out_b[l], w_gu_b[l], w_down_b[l], tm=256)
        ka_l.append(ka.reshape(nbp, seq_p, A_KV_HEADS, HEAD_DIM))
        va_l.append(za[:, A_WIDTH + A_KV_WIDTH:].reshape(nbp, seq_p, A_KV_HEADS, HEAD_DIM))
        sb_l.append(s_b)
        kc_l.append(zc[:, C_WIDTH:2 * C_WIDTH].reshape(nbp, seq_p, C_HEADS, HEAD_DIM))
        vc_l.append(zc[:, 2 * C_WIDTH:].reshape(nbp, seq_p, C_HEADS, HEAD_DIM))

        za, zb, zc = _inproj(xs, mod_s, g0, w_in_p[l], tm=256)
        (o_a,) = _attention(za, nb=nbs, seq=seq_s, n_q=A_HEADS, n_kv=A_KV_HEADS, tq=256, layer=l,
                            kc=ck_a, vc=cv_a, gq=gq, gk=gk, cos=cos, sin=sin)
        o_b, _ = _deltanet(zb, state_b, conv_w[l], a_log[l], dt_bias[l], g_onorm_b[l], nb=nbs, seq=seq_s, layer=l)
        o_c = _nbr_attention(zc, ck_c, cv_c, tt, nb=nbs, seq=seq_s, layer=l)
        xs = _post(xs, o_a, o_b, o_c, mod_s, g_norm[l], goa, goc, w_out_b[l], w_gu_b[l], w_down_b[l], tm=256)

    return (xp.reshape(nbp, seq_p, d), xs.reshape(nbs, seq_s, d),
            jnp.stack(ka_l, axis=1), jnp.stack(va_l, axis=1), jnp.stack(sb_l, axis=1),
            jnp.stack(kc_l, axis=1), jnp.stack(vc_l, axis=1))
```

```python
import functools

import numpy as np
import jax
import jax.numpy as jnp
from jax import lax
from jax.experimental import pallas as pl
from jax.experimental.pallas import tpu as pltpu

F32 = jnp.float32
BF16 = jnp.bfloat16

D_MODEL = 1024
HEAD_DIM = 64
GRID_W = 64
A_HEADS, A_KV_HEADS, B_HEADS, C_HEADS = 6, 2, 4, 6
A_WIDTH, A_KV_WIDTH = A_HEADS * HEAD_DIM, A_KV_HEADS * HEAD_DIM
B_WIDTH, C_WIDTH = B_HEADS * HEAD_DIM, C_HEADS * HEAD_DIM
N_DIR = 2
CHUNK = 64
WIN_R, WIN_C = 8, 16
ROPE_THETA = 10000.0
D_FF = 2816
N_MOD = 6
EPS = 1e-6
LANES = 128
ZA_W = A_WIDTH + 2 * A_KV_WIDTH
ZB_W = 3 * B_WIDTH + B_WIDTH + LANES
ZC_W = 3 * C_WIDTH
GATE_OFF = 4 * B_WIDTH
VMEM_LIMIT = 56 * 1024 * 1024
TOKEN_TILE = 256
Q_TILE = 256
NBR_G = 8
NBR_KROWS = 16
NEG_INF = float("-inf")


def _cparams(sem):
    return pltpu.CompilerParams(dimension_semantics=sem, vmem_limit_bytes=VMEM_LIMIT)


def _split2(x):
    hi = x.astype(BF16)
    lo = (x - hi.astype(F32)).astype(BF16)
    return hi, lo


def _split3(x):
    hi = x.astype(BF16)
    r1 = x - hi.astype(F32)
    mid = r1.astype(BF16)
    lo = (r1 - mid.astype(F32)).astype(BF16)
    return hi, mid, lo


def _dot(a, b):
    return jnp.dot(a, b, preferred_element_type=F32)


def _nt(a, b):
    return lax.dot_general(a, b, (((1,), (1,)), ((), ())), preferred_element_type=F32)


def _tn(a, b):
    return lax.dot_general(a, b, (((0,), (0,)), ((), ())), preferred_element_type=F32)


def _head_ones(width):
    r = lax.broadcasted_iota(jnp.int32, (width, width), 0) // HEAD_DIM
    c = lax.broadcasted_iota(jnp.int32, (width, width), 1) // HEAD_DIM
    return jnp.where(r == c, 1.0, 0.0).astype(BF16)


def _head_sum(x, ones):
    hi, lo = _split2(x)
    return _dot(hi, ones) + _dot(lo, ones)


def _head_rmsnorm(x, g_row, ones):
    ms = _head_sum(x * x, ones) * (1.0 / HEAD_DIM)
    return x * lax.rsqrt(ms + EPS) * g_row


def _rmsnorm(x, g_row):
    ms = jnp.mean(x * x, axis=-1, keepdims=True)
    return x * lax.rsqrt(ms + EPS) * g_row


def _silu(x):
    return x * jax.nn.sigmoid(x)


def _rope(x, cos, sin_signed, width):
    reps = width // LANES
    if reps > 1:
        cos = jnp.concatenate([cos] * reps, axis=1)
        sin_signed = jnp.concatenate([sin_signed] * reps, axis=1)
    lane = lax.broadcasted_iota(jnp.int32, x.shape, 1)
    first = (lane & (HEAD_DIM // 2)) == 0
    partner = jnp.where(first, pltpu.roll(x, width - HEAD_DIM // 2, 1), pltpu.roll(x, HEAD_DIM // 2, 1))
    return x * cos + partner * sin_signed


def _mod_kernel(c_ref, w_ref, b_ref, o_ref):
    a = _silu(c_ref[...]).astype(BF16)
    o_ref[0] = _dot(a, w_ref[0].astype(BF16)) + b_ref[0]


def _modulation(cvec8, w_mod, b_mod):
    depth, d, n = w_mod.shape
    tn = 1536
    return pl.pallas_call(
        _mod_kernel,
        out_shape=jax.ShapeDtypeStruct((depth, 8, n), F32),
        grid=(depth, n // tn),
        in_specs=[pl.BlockSpec((8, d), lambda l, j: (0, 0)),
                  pl.BlockSpec((1, d, tn), lambda l, j: (l, 0, j)),
                  pl.BlockSpec((1, 1, tn), lambda l, j: (l, 0, j))],
        out_specs=pl.BlockSpec((1, 8, tn), lambda l, j: (l, 0, j)),
        compiler_params=_cparams(("arbitrary", "arbitrary")),
        name="modulation",
    )(cvec8, w_mod, b_mod.reshape(depth, 1, n))


def _inproj_kernel(x_ref, mod_ref, g_ref, w_ref, za_ref, zb_ref, zc_ref):
    x = x_ref[...]
    h = _rmsnorm(x, g_ref[...]) * (1.0 + mod_ref[0, 1:2, :]) + mod_ref[0, 0:1, :]
    z = _dot(h.astype(BF16), w_ref[...])
    za_ref[...] = z[:, 0:ZA_W]
    zb_ref[...] = z[:, ZA_W:ZA_W + ZB_W]
    zc_ref[...] = z[:, ZA_W + ZB_W:]


def _inproj(x, mod, g_row, w):
    m, d = x.shape
    tm = TOKEN_TILE
    nb = mod.shape[0]
    per = (m // nb) // tm
    n = w.shape[1]
    return pl.pallas_call(
        _inproj_kernel,
        out_shape=(jax.ShapeDtypeStruct((m, ZA_W), F32),
                   jax.ShapeDtypeStruct((m, ZB_W), F32),
                   jax.ShapeDtypeStruct((m, ZC_W), F32)),
        grid=(m // tm,),
        in_specs=[pl.BlockSpec((tm, d), lambda i: (i, 0)),
                  pl.BlockSpec((1, N_MOD, d), lambda i: (i // per, 0, 0)),
                  pl.BlockSpec((1, d), lambda i: (0, 0)),
                  pl.BlockSpec((d, n), lambda i: (0, 0), pipeline_mode=pl.Buffered(1))],
        out_specs=(pl.BlockSpec((tm, ZA_W), lambda i: (i, 0)),
                   pl.BlockSpec((tm, ZB_W), lambda i: (i, 0)),
                   pl.BlockSpec((tm, ZC_W), lambda i: (i, 0))),
        compiler_params=_cparams(("arbitrary",)),
        name="inproj",
    )(x, mod, g_row, w)


def _attn_kernel(*refs, n_q, n_kv, tq, l_self, l_ctx, norm, rope, emit_k):
    it = iter(refs)
    q_ref, k_ref, v_ref = next(it), next(it), next(it)
    kc_ref = vc_ref = gq_ref = gk_ref = cos_ref = sin_ref = kout_ref = None
    if l_ctx:
        kc_ref, vc_ref = next(it), next(it)
    if norm:
        gq_ref, gk_ref = next(it), next(it)
    if rope:
        cos_ref, sin_ref = next(it), next(it)
    o_ref = next(it)
    if emit_k:
        kout_ref = next(it)
    kbuf, vbuf = next(it), next(it)

    j = pl.program_id(1)
    wq, wk = n_q * HEAD_DIM, n_kv * HEAD_DIM
    grp = n_q // n_kv
    rb = 256

    @pl.when(j == 0)
    def _():
        ones_k = _head_ones(wk) if norm else None

        def body(i, carry):
            r0 = pl.multiple_of(i * rb, rb)
            k = k_ref[pl.ds(r0, rb), :]
            if norm:
                k = _head_rmsnorm(k, gk_ref[...], ones_k)
            if rope:
                k = _rope(k, cos_ref[pl.ds(r0, rb), :], sin_ref[pl.ds(r0, rb), :], wk)
            if emit_k:
                kout_ref[pl.ds(r0, rb), :] = k
            kbuf[pl.ds(r0, rb), :] = k.astype(BF16)
            vbuf[pl.ds(r0, rb), :] = v_ref[pl.ds(r0, rb), :].astype(BF16)
            return carry
        lax.fori_loop(0, l_self // rb, body, 0)
        if l_ctx:
            kbuf[l_self:l_self + l_ctx, :] = kc_ref[...].astype(BF16)
            vbuf[l_self:l_self + l_ctx, :] = vc_ref[...].astype(BF16)

    q = q_ref[...]
    if norm:
        q = _head_rmsnorm(q, gq_ref[...], _head_ones(wq))
    if rope:
        r0 = pl.multiple_of(j * tq, tq)
        q = _rope(q, cos_ref[pl.ds(r0, tq), :], sin_ref[pl.ds(r0, tq), :], wq)
    q = (q * (HEAD_DIM ** -0.5)).astype(BF16)
    for h in range(n_q):
        kv = h // grp
        qh = q[:, h * HEAD_DIM:(h + 1) * HEAD_DIM]
        s = _nt(qh, kbuf[:, kv * HEAD_DIM:(kv + 1) * HEAD_DIM])
        m = jnp.max(s, axis=-1, keepdims=True)
        p = jnp.exp(s - m)
        l = jnp.sum(p, axis=-1, keepdims=True)
        o = _dot(p.astype(BF16), vbuf[:, kv * HEAD_DIM:(kv + 1) * HEAD_DIM])
        o_ref[:, h * HEAD_DIM:(h + 1) * HEAD_DIM] = o / l


def _attention(z, *, nb, seq, n_q, n_kv, tq, layer=None, kc=None, vc=None, gq=None, gk=None,
               cos=None, sin=None, emit_k=False):
    wq, wk = n_q * HEAD_DIM, n_kv * HEAD_DIM
    nq = seq // tq
    l_ctx = 0 if kc is None else kc.shape[2]
    norm, rope = gq is not None, cos is not None
    kblk, vblk = wq // wk, wq // wk + 1
    args = [z, z, z]
    in_specs = [pl.BlockSpec((tq, wq), lambda b, j: (b * nq + j, 0)),
                pl.BlockSpec((seq, wk), lambda b, j: (b, kblk)),
                pl.BlockSpec((seq, wk), lambda b, j: (b, vblk))]
    if l_ctx:
        args += [kc, vc]
        in_specs += [pl.BlockSpec((None, None, l_ctx, wk), lambda b, j: (b, layer, 0, 0))] * 2
    if norm:
        args += [gq, gk]
        in_specs += [pl.BlockSpec((1, wq), lambda b, j: (0, 0)), pl.BlockSpec((1, wk), lambda b, j: (0, 0))]
    if rope:
        args += [cos, sin]
        in_specs += [pl.BlockSpec((seq, LANES), lambda b, j: (0, 0))] * 2
    out_shape = [jax.ShapeDtypeStruct((nb * seq, wq), F32)]
    out_specs = [pl.BlockSpec((tq, wq), lambda b, j: (b * nq + j, 0))]
    if emit_k:
        out_shape.append(jax.ShapeDtypeStruct((nb * seq, wk), F32))
        out_specs.append(pl.BlockSpec((seq, wk), lambda b, j: (b, 0)))
    kern = functools.partial(_attn_kernel, n_q=n_q, n_kv=n_kv, tq=tq, l_self=seq, l_ctx=l_ctx,
                             norm=norm, rope=rope, emit_k=emit_k)
    return pl.pallas_call(
        kern,
        out_shape=tuple(out_shape),
        grid=(nb, nq),
        in_specs=in_specs,
        out_specs=tuple(out_specs),
        scratch_shapes=[pltpu.VMEM((seq + l_ctx, wk), BF16), pltpu.VMEM((seq + l_ctx, wk), BF16)],
        compiler_params=_cparams(("arbitrary", "arbitrary")),
        name="attention",
    )(*args)


def _nbr_plan(rows):
    nblk = rows // NBR_G
    kstart, pairs, ent = [], {}, []
    for blk in range(nblk):
        r0 = blk * NBR_G
        ks = min(max(r0 - WIN_R // 2, 0), rows - NBR_KROWS)
        kstart.append(ks)
        for i in range(NBR_G):
            r = r0 + i
            rs = min(max(r - WIN_R // 2, 0), rows - WIN_R)
            assert ks <= rs and rs + WIN_R <= ks + NBR_KROWS
            for p in range(NBR_KROWS // 2):
                ab = []
                for krow in (ks + 2 * p, ks + 2 * p + 1):
                    valid = rs <= krow < rs + WIN_R
                    ab.append(krow - r + WIN_R if valid else 0)
                ent.append(pairs.setdefault(tuple(ab), len(pairs)))
    pair_list = sorted(pairs, key=pairs.get)
    return (np.asarray(kstart, np.int32), np.asarray(ent, np.int32),
            np.asarray([a for a, _ in pair_list], np.int32), np.asarray([b for _, b in pair_list], np.int32))


def _nbr_bias_tables(rpb, rows):
    _, _, left, right = _nbr_plan(rows)
    col = np.arange(GRID_W)
    c_start = np.clip(col - WIN_C // 2, 0, GRID_W - WIN_C)
    cmask = (col[None, :] >= c_start[:, None]) & (col[None, :] < c_start[:, None] + WIN_C)
    dc = np.clip(col[None, :] - col[:, None] + (WIN_C - 1), 0, 2 * WIN_C - 2)
    t = jnp.take(rpb.astype(F32), jnp.asarray(dc.reshape(-1)), axis=-1)
    t = t.reshape(rpb.shape[:3] + (GRID_W, GRID_W))
    t = jnp.where(jnp.asarray(cmask), t, NEG_INF)
    masked = jnp.full(t.shape[:2] + (1, GRID_W, GRID_W), NEG_INF, F32)
    tpad = jnp.concatenate([masked, t], axis=2)
    return jnp.concatenate([jnp.take(tpad, jnp.asarray(left), axis=2),
                            jnp.take(tpad, jnp.asarray(right), axis=2)], axis=-1)


def _nbr_kernel(ks_ref, ent_ref, q_ref, k_ref, v_ref, kc_ref, vc_ref, tt_ref, o_ref,
                kbuf, vbuf, kcbuf, vcbuf, s_ref, *, seq):
    blk = pl.program_id(1)
    nk = NBR_KROWS * GRID_W
    npair = NBR_KROWS // 2
    rb = 256

    @pl.when(blk == 0)
    def _():
        def body(i, carry):
            r0 = pl.multiple_of(i * rb, rb)
            kbuf[pl.ds(r0, rb), :] = k_ref[pl.ds(r0, rb), :].astype(BF16)
            vbuf[pl.ds(r0, rb), :] = v_ref[pl.ds(r0, rb), :].astype(BF16)
            return carry
        lax.fori_loop(0, seq // rb, body, 0)
        kcbuf[...] = kc_ref[...].astype(BF16)
        vcbuf[...] = vc_ref[...].astype(BF16)

    ks = pl.multiple_of(ks_ref[blk] * GRID_W, GRID_W)
    q = (q_ref[...] * (HEAD_DIM ** -0.5)).astype(BF16)
    for h in range(C_HEADS):
        hs = slice(h * HEAD_DIM, (h + 1) * HEAD_DIM)
        qh = q[:, hs]
        s_ref[...] = _nt(qh, kbuf[pl.ds(ks, nk), hs])
        for i in range(NBR_G):
            for p in range(npair):
                e = ent_ref[(blk * NBR_G + i) * npair + p]
                rs_, cs_ = slice(i * GRID_W, (i + 1) * GRID_W), slice(p * LANES, (p + 1) * LANES)
                s_ref[rs_, cs_] = s_ref[rs_, cs_] + tt_ref[h, e]
        s_loc = s_ref[...]
        s_ctx = _nt(qh, kcbuf[:, hs])
        m = jnp.maximum(jnp.max(s_loc, axis=-1, keepdims=True), jnp.max(s_ctx, axis=-1, keepdims=True))
        p_loc = jnp.exp(s_loc - m)
        p_ctx = jnp.exp(s_ctx - m)
        l = jnp.sum(p_loc, axis=-1, keepdims=True) + jnp.sum(p_ctx, axis=-1, keepdims=True)
        o = _dot(p_loc.astype(BF16), vbuf[pl.ds(ks, nk), hs]) + _dot(p_ctx.astype(BF16), vcbuf[:, hs])
        o_ref[:, hs] = o / l


def _nbr_attention(zc, kc, vc, tt, *, nb, seq, layer):
    rows = seq // GRID_W
    kstart, ent, _, _ = _nbr_plan(rows)
    nblk = rows // NBR_G
    tq = NBR_G * GRID_W
    nk = NBR_KROWS * GRID_W
    l_ctx = kc.shape[2]
    n_ent = tt.shape[2]
    smem = pl.BlockSpec(memory_space=pltpu.SMEM)
    return pl.pallas_call(
        functools.partial(_nbr_kernel, seq=seq),
        out_shape=jax.ShapeDtypeStruct((nb * seq, C_WIDTH), F32),
        grid=(nb, nblk),
        in_specs=[smem, smem,
                  pl.BlockSpec((tq, C_WIDTH), lambda b, j: (b * nblk + j, 0)),
                  pl.BlockSpec((seq, C_WIDTH), lambda b, j: (b, 1)),
                  pl.BlockSpec((seq, C_WIDTH), lambda b, j: (b, 2)),
                  pl.BlockSpec((None, None, l_ctx, C_WIDTH), lambda b, j: (b, layer, 0, 0)),
                  pl.BlockSpec((None, None, l_ctx, C_WIDTH), lambda b, j: (b, layer, 0, 0)),
                  pl.BlockSpec((None, C_HEADS, n_ent, GRID_W, LANES), lambda b, j: (layer, 0, 0, 0, 0))],
        out_specs=pl.BlockSpec((tq, C_WIDTH), lambda b, j: (b * nblk + j, 0)),
        scratch_shapes=[pltpu.VMEM((seq, C_WIDTH), BF16), pltpu.VMEM((seq, C_WIDTH), BF16),
                        pltpu.VMEM((l_ctx, C_WIDTH), BF16), pltpu.VMEM((l_ctx, C_WIDTH), BF16),
                        pltpu.VMEM((tq, nk), F32)],
        compiler_params=_cparams(("arbitrary", "arbitrary")),
        name="nbr_attention",
    )(jnp.asarray(kstart), jnp.asarray(ent), zc, zc, zc, kc, vc, tt)


def _bd(x, mask):
    return jnp.where(mask, jnp.concatenate([x] * B_HEADS, axis=0), jnp.zeros((), x.dtype))


def _unpack_bd(full, lane_head):
    out = None
    for g in range(B_HEADS):
        blk = jnp.where(lane_head == g, full[g * HEAD_DIM:(g + 1) * HEAD_DIM, :], 0.0)
        out = blk if out is None else out + blk
    return out


def _pk_mm(a, b, mask):
    ah, al = _split2(a)
    bh, bl = _split2(b)
    top = _dot(jnp.concatenate([ah, al], axis=0), _bd(bh, mask))
    return top[0:CHUNK] + top[CHUNK:] + _dot(ah, _bd(bl, mask))


def _pk_unit_tri_inverse(lmat, lower, ri, cj, mask):
    t = jnp.where(ri == cj, 1.0, 0.0).astype(F32)
    s = 1
    while s < CHUNK:
        same_pair = ((ri ^ cj) & ~(2 * s - 1)) == 0
        r_hi, c_hi = (ri & s) != 0, (cj & s) != 0
        sel = same_pair & ((r_hi & ~c_hi) if lower else (~r_hi & c_hi))
        off = jnp.where(sel, lmat, 0.0)
        t = t - off if s == 1 else t - _pk_mm(_pk_mm(t, off, mask), t, mask)
        s *= 2
    return t


def _delta_kernel(*refs, seq, has_s0):
    it = iter(refs)
    zb_ref = next(it)
    s0_ref = next(it) if has_s0 else None
    convw_ref, alog_ref, dtb_ref, gon_ref = next(it), next(it), next(it), next(it)
    o_ref, sfin_ref = next(it), next(it)
    qt_s, au_s, p_s, n_s, gam_s, of_s, ob_s, st_s = (next(it) for _ in range(8))

    n = seq // CHUNK
    c64 = CHUNK
    pw = B_WIDTH
    row = lax.broadcasted_iota(jnp.int32, (c64, 1), 0)
    lane = lax.broadcasted_iota(jnp.int32, (c64, LANES), 1)
    ri = lax.broadcasted_iota(jnp.int32, (c64, pw), 0)
    lane_p = lax.broadcasted_iota(jnp.int32, (c64, pw), 1)
    cj = lane_p & (HEAD_DIM - 1)
    lane_head = lane_p // HEAD_DIM
    bd_mask = (lax.broadcasted_iota(jnp.int32, (pw, pw), 0) // HEAD_DIM
               == lax.broadcasted_iota(jnp.int32, (pw, pw), 1) // HEAD_DIM)
    ones_b = _head_ones(pw)
    w0, w1, w2 = convw_ref[0:1, :], convw_ref[1:2, :], convw_ref[2:3, :]
    neg_decay_rate = -jnp.exp(alog_ref[...])
    eye8 = jnp.where(lax.broadcasted_iota(jnp.int32, (8, LANES), 0) == lax.broadcasted_iota(jnp.int32, (8, LANES), 1),
                     1.0, 0.0).astype(BF16)
    expand = jnp.where(lax.broadcasted_iota(jnp.int32, (LANES, N_DIR * pw), 1) // HEAD_DIM
                       == lax.broadcasted_iota(jnp.int32, (LANES, N_DIR * pw), 0), 1.0, 0.0).astype(BF16)

    def chunk_terms(c):
        r0 = pl.multiple_of(c * c64, c64)
        x = zb_ref[pl.ds(r0, c64), 0:3 * B_WIDTH]
        prev = zb_ref[pl.ds(jnp.maximum(r0 - 1, 0), 1), 0:3 * B_WIDTH] * (c > 0).astype(F32)
        nxt = zb_ref[pl.ds(jnp.minimum(r0 + c64, seq - 1), 1), 0:3 * B_WIDTH] * (c < n - 1).astype(F32)
        x_up = jnp.where(row == 0, prev, pltpu.roll(x, 1, 0))
        x_dn = jnp.where(row == c64 - 1, nxt, pltpu.roll(x, c64 - 1, 0))
        y = _silu(x_up * w0 + x * w1 + x_dn * w2)
        q, k, v = y[:, 0:pw], y[:, pw:2 * pw], y[:, 2 * pw:]
        q = q * lax.rsqrt(_head_sum(q * q, ones_b) + EPS) * (HEAD_DIM ** -0.5)
        k = k * lax.rsqrt(_head_sum(k * k, ones_b) + EPS)

        ga = zb_ref[pl.ds(r0, c64), GATE_OFF:GATE_OFF + LANES]
        beta = jax.nn.sigmoid(ga)
        a = pltpu.roll(ga, LANES - N_DIR * B_HEADS, 1) + dtb_ref[...]
        g = neg_decay_rate * (jnp.maximum(a, 0.0) + jnp.log(1.0 + jnp.exp(-jnp.abs(a))))
        cf, cr = g, g
        sh = 1
        while sh < c64:
            cf = cf + jnp.where(row >= sh, pltpu.roll(cf, sh, 0), 0.0)
            cr = cr + jnp.where(row < c64 - sh, pltpu.roll(cr, c64 - sh, 0), 0.0)
            sh *= 2
        gc = jnp.where(lane < B_HEADS, cf, cr)
        g3 = _split3(gc)
        gct = _nt(eye8, g3[0]) + (_nt(eye8, g3[1]) + _nt(eye8, g3[2]))
        b2 = _split2(beta)
        ex = _dot(jnp.concatenate([b2[0], b2[1], g3[0], g3[1], g3[2]], axis=0), expand)
        bexp = ex[0:c64] + ex[c64:2 * c64]
        gexp = ex[2 * c64:3 * c64] + (ex[3 * c64:4 * c64] + ex[4 * c64:])

        k_bd = _bd(k.astype(BF16), bd_mask)
        q_bf = q.astype(BF16)
        for d in range(N_DIR):
            ls = slice(d * pw, (d + 1) * pw)
            bx, gx = bexp[:, ls], gexp[:, ls]
            incl = (ri >= cj) if d == 0 else (ri <= cj)
            strict = (ri > cj) if d == 0 else (ri < cj)
            last = c64 - 1 if d == 0 else 0
            g_row = jnp.concatenate([gct[d * B_HEADS + h:d * B_HEADS + h + 1, :] for h in range(B_HEADS)], axis=1)
            g_last = gx[last:last + 1, :]
            decay = jnp.exp(jnp.where(incl, gx - g_row, NEG_INF))
            kb = k * bx
            aa = _nt(jnp.concatenate([kb.astype(BF16), q_bf], axis=0), k_bd)
            lmat = jnp.where(strict, aa[0:c64] * decay, 0.0)
            a_intra = jnp.where(incl, aa[c64:] * decay, 0.0).astype(BF16)
            tinv = _pk_unit_tri_inverse(lmat, d == 0, ri, cj, bd_mask)
            eg = jnp.exp(gx)
            u = _pk_mm(tinv, v * bx, bd_mask).astype(BF16)
            w = _pk_mm(tinv, kb * eg, bd_mask).astype(BF16)
            qt_s[c, d] = (q * eg - _dot(a_intra, _bd(w, bd_mask))).astype(BF16)
            au_s[c, d] = _dot(a_intra, _bd(u, bd_mask))
            k_dec = (k * jnp.exp(g_last - gx)).astype(BF16)
            full = _tn(k_dec, jnp.concatenate([w, u], axis=1))
            p_s[c, d] = _unpack_bd(full[:, 0:pw], lane_head).astype(BF16)
            n_s[c, d] = _unpack_bd(full[:, pw:], lane_head)
            gam_s[c, d] = jnp.broadcast_to(jnp.exp(g_last), (8, pw))

    def terms_loop(c, carry):
        chunk_terms(c)
        return carry

    lax.fori_loop(0, n, terms_loop, 0)

    for d in range(N_DIR):
        if has_s0:
            s_pk = jnp.concatenate([s0_ref[d, h] for h in range(B_HEADS)], axis=1)
            st_s[d] = _bd(s_pk, bd_mask)
        else:
            st_s[d] = jnp.zeros((pw, pw), F32)

    def scan(t, carry):
        for d in range(N_DIR):
            c = t if d == 0 else n - 1 - t
            r0 = pl.multiple_of(c * c64, c64)
            s = st_s[d]
            s_bf = s.astype(BF16)
            out_s = of_s if d == 0 else ob_s
            out_s[pl.ds(r0, c64), :] = _dot(qt_s[c, d], s_bf) + au_s[c, d]
            st_s[d] = (s * gam_s[c, d, 0:1, :] - _dot(_bd(p_s[c, d], bd_mask), s_bf)) + _bd(n_s[c, d], bd_mask)
        return carry

    lax.fori_loop(0, n, scan, 0)

    def finish(c, carry):
        r0 = pl.multiple_of(c * c64, c64)
        o = of_s[pl.ds(r0, c64), :] + ob_s[pl.ds(r0, c64), :]
        y = _head_rmsnorm(o, gon_ref[...], ones_b)
        o_ref[pl.ds(r0, c64), :] = y * _silu(zb_ref[pl.ds(r0, c64), 3 * B_WIDTH:4 * B_WIDTH])
        return carry

    lax.fori_loop(0, n, finish, 0)
    for d in range(N_DIR):
        s_pk = _unpack_bd(st_s[d], lax.broadcasted_iota(jnp.int32, (HEAD_DIM, pw), 1) // HEAD_DIM)
        for h in range(B_HEADS):
            sfin_ref[d, h] = s_pk[:, h * HEAD_DIM:(h + 1) * HEAD_DIM]


def _deltanet(zb, s0, conv_w, a_log, dt_bias, g_onorm, *, nb, seq, layer=None):
    n = seq // CHUNK
    has_s0 = s0 is not None
    pad = LANES - N_DIR * B_HEADS
    alog_row = jnp.pad(a_log.reshape(1, -1).astype(F32), ((0, 0), (0, pad)))
    dtb_row = jnp.pad(dt_bias.reshape(1, -1).astype(F32), ((0, 0), (0, pad)))
    gon_row = jnp.tile(g_onorm.astype(F32), B_HEADS).reshape(1, B_WIDTH)
    st_shape = (N_DIR, B_HEADS, HEAD_DIM, HEAD_DIM)
    args = [zb]
    in_specs = [pl.BlockSpec((seq, ZB_W), lambda b: (b, 0))]
    if has_s0:
        args.append(s0)
        in_specs.append(pl.BlockSpec((None, None) + st_shape, lambda b: (b, layer, 0, 0, 0, 0)))
    args += [conv_w.astype(F32), alog_row, dtb_row, gon_row]
    in_specs += [pl.BlockSpec((3, 3 * B_WIDTH), lambda b: (0, 0)),
                 pl.BlockSpec((1, LANES), lambda b: (0, 0)),
                 pl.BlockSpec((1, LANES), lambda b: (0, 0)),
                 pl.BlockSpec((1, B_WIDTH), lambda b: (0, 0))]
    per_chunk = (n, N_DIR, CHUNK, B_WIDTH)
    return pl.pallas_call(
        functools.partial(_delta_kernel, seq=seq, has_s0=has_s0),
        out_shape=(jax.ShapeDtypeStruct((nb * seq, B_WIDTH), F32),
                   jax.ShapeDtypeStruct((nb,) + st_shape, F32)),
        grid=(nb,),
        in_specs=in_specs,
        out_specs=(pl.BlockSpec((seq, B_WIDTH), lambda b: (b, 0)),
                   pl.BlockSpec((None,) + st_shape, lambda b: (b, 0, 0, 0, 0))),
        scratch_shapes=[pltpu.VMEM(per_chunk, BF16), pltpu.VMEM(per_chunk, F32),
                        pltpu.VMEM(per_chunk, BF16), pltpu.VMEM(per_chunk, F32),
                        pltpu.VMEM((n, N_DIR, 8, B_WIDTH), F32),
                        pltpu.VMEM((seq, B_WIDTH), F32), pltpu.VMEM((seq, B_WIDTH), F32),
                        pltpu.VMEM((N_DIR, B_WIDTH, B_WIDTH), F32)],
        compiler_params=_cparams(("arbitrary",)),
        name="deltanet",
    )(*args)


def _post_kernel(x_ref, oa_ref, ob_ref, oc_ref, mod_ref, gn_ref, ga_ref, gc_ref,
                 wo_ref, wgu_ref, wd_ref, y_ref):
    x = x_ref[...]
    ya = _rmsnorm(oa_ref[...], ga_ref[...])
    yc = _rmsnorm(oc_ref[...], gc_ref[...])
    y = jnp.concatenate([ya, ob_ref[...], yc], axis=-1).astype(BF16)
    m = _dot(y, wo_ref[...])
    x1 = x + mod_ref[0, 2:3, :] * _rmsnorm(m, gn_ref[1:2, :])
    h = _rmsnorm(x1, gn_ref[2:3, :]) * (1.0 + mod_ref[0, 4:5, :]) + mod_ref[0, 3:4, :]
    gu = _dot(h.astype(BF16), wgu_ref[...])
    act = (_silu(gu[:, 0:D_FF]) * gu[:, D_FF:]).astype(BF16)
    f = _dot(act, wd_ref[...])
    y_ref[...] = x1 + mod_ref[0, 5:6, :] * _rmsnorm(f, gn_ref[3:4, :])


def _post(x, oa, ob, oc, mod, g_norm, g_out_a, g_out_c, w_out, w_gu, w_down):
    m, d = x.shape
    tm = TOKEN_TILE
    nb = mod.shape[0]
    per = (m // nb) // tm
    row = lambda i: (i, 0)
    const = lambda i: (0, 0)
    single = pl.Buffered(1)
    return pl.pallas_call(
        _post_kernel,
        out_shape=jax.ShapeDtypeStruct((m, d), F32),
        grid=(m // tm,),
        in_specs=[pl.BlockSpec((tm, d), row),
                  pl.BlockSpec((tm, A_WIDTH), row),
                  pl.BlockSpec((tm, B_WIDTH), row),
                  pl.BlockSpec((tm, C_WIDTH), row),
                  pl.BlockSpec((1, N_MOD, d), lambda i: (i // per, 0, 0)),
                  pl.BlockSpec((4, d), const),
                  pl.BlockSpec((1, A_WIDTH), const),
                  pl.BlockSpec((1, C_WIDTH), const),
                  pl.BlockSpec(w_out.shape, const, pipeline_mode=single),
                  pl.BlockSpec(w_gu.shape, const, pipeline_mode=single),
                  pl.BlockSpec(w_down.shape, const, pipeline_mode=single)],
        out_specs=pl.BlockSpec((tm, d), row),
        compiler_params=_cparams(("arbitrary",)),
        name="post",
    )(x, oa, ob, oc, mod, g_norm, g_out_a, g_out_c, w_out, w_gu, w_down)


def _rope_tables(seq):
    quarter = HEAD_DIM // 4
    inv = ROPE_THETA ** (-jnp.arange(quarter, dtype=F32) / quarter)
    t = jnp.arange(seq)
    rowp = (t // GRID_W).astype(F32)
    colp = (t % GRID_W).astype(F32)
    ang = jnp.concatenate([rowp[:, None] * inv, colp[:, None] * inv], axis=-1)
    cos, sin = jnp.cos(ang), jnp.sin(ang)
    reps = LANES // HEAD_DIM
    return (jnp.tile(jnp.concatenate([cos, cos], axis=-1), (1, reps)),
            jnp.tile(jnp.concatenate([-sin, sin], axis=-1), (1, reps)))


def _permute_w_in(w_in):
    cut = ZA_W + 4 * B_WIDTH + 2 * N_DIR * B_HEADS
    pad = jnp.zeros(w_in.shape[:-1] + (LANES - 2 * N_DIR * B_HEADS,), w_in.dtype)
    return jnp.concatenate([w_in[..., :cut], pad, w_in[..., cut:]], axis=-1).astype(BF16)


def kernel(x_prompt, x_sample, cache_a_k, cache_a_v, state_b, cache_c_k, cache_c_v, c, c_ctx, w_mod, b_mod,
           g_norm, w_in, g_qk_a, g_out_a, conv_w, a_log, dt_bias, g_onorm_b, rpb, g_out_c, w_out, w_gu, w_down):
    nbp, seq_p, d = x_prompt.shape
    nbs, seq_s, _ = x_sample.shape
    depth = w_mod.shape[0]
    past = cache_a_k.shape[2]

    cvec = jnp.concatenate([c_ctx[None, :], c, jnp.zeros((8 - 1 - nbs, d), F32)], axis=0)
    mods = _modulation(cvec, w_mod, b_mod)
    w_in_p = _permute_w_in(w_in)
    w_out_b, w_gu_b, w_down_b = w_out.astype(BF16), w_gu.astype(BF16), w_down.astype(BF16)
    cos, sin = _rope_tables(seq_s)
    tt = _nbr_bias_tables(rpb, seq_s // GRID_W)
    ck_a = cache_a_k.reshape(nbs, depth, past, A_KV_WIDTH)
    cv_a = cache_a_v.reshape(nbs, depth, past, A_KV_WIDTH)
    ck_c = cache_c_k.reshape(nbs, depth, past, C_WIDTH)
    cv_c = cache_c_v.reshape(nbs, depth, past, C_WIDTH)

    xp = x_prompt.reshape(nbp * seq_p, d)
    xs = x_sample.reshape(nbs * seq_s, d)
    ka_l, va_l, sb_l, kc_l, vc_l = [], [], [], [], []
    for l in range(depth):
        gq = jnp.tile(g_qk_a[l, 0], A_HEADS).reshape(1, A_WIDTH)
        gk = jnp.tile(g_qk_a[l, 1], A_KV_HEADS).reshape(1, A_KV_WIDTH)
        goa, goc = g_out_a[l].reshape(1, A_WIDTH), g_out_c[l].reshape(1, C_WIDTH)
        g0 = g_norm[l, 0:1]
        mod_p = mods[l, 0:1].reshape(1, N_MOD, d)
        mod_s = mods[l, 1:1 + nbs].reshape(nbs, N_MOD, d)

        za, zb, zc = _inproj(xp, mod_p, g0, w_in_p[l])
        o_a, ka = _attention(za, nb=nbp, seq=seq_p, n_q=A_HEADS, n_kv=A_KV_HEADS, tq=seq_p,
                             gq=gq, gk=gk, emit_k=True)
        o_b, s_b = _deltanet(zb, None, conv_w[l], a_log[l], dt_bias[l], g_onorm_b[l], nb=nbp, seq=seq_p)
        (o_c,) = _attention(zc, nb=nbp, seq=seq_p, n_q=C_HEADS, n_kv=C_HEADS, tq=seq_p)
        xp = _post(xp, o_a, o_b, o_c, mod_p, g_norm[l], goa, goc, w_out_b[l], w_gu_b[l], w_down_b[l])
        ka_l.append(ka.reshape(nbp, seq_p, A_KV_HEADS, HEAD_DIM))
        va_l.append(za[:, A_WIDTH + A_KV_WIDTH:].reshape(nbp, seq_p, A_KV_HEADS, HEAD_DIM))
        sb_l.append(s_b)
        kc_l.append(zc[:, C_WIDTH:2 * C_WIDTH].reshape(nbp, seq_p, C_HEADS, HEAD_DIM))
        vc_l.append(zc[:, 2 * C_WIDTH:].reshape(nbp, seq_p, C_HEADS, HEAD_DIM))

        za, zb, zc = _inproj(xs, mod_s, g0, w_in_p[l])
        (o_a,) = _attention(za, nb=nbs, seq=seq_s, n_q=A_HEADS, n_kv=A_KV_HEADS, tq=Q_TILE, layer=l,
                            kc=ck_a, vc=cv_a, gq=gq, gk=gk, cos=cos, sin=sin)
        o_b, _ = _deltanet(zb, state_b, conv_w[l], a_log[l], dt_bias[l], g_onorm_b[l], nb=nbs, seq=seq_s, layer=l)
        o_c = _nbr_attention(zc, ck_c, cv_c, tt, nb=nbs, seq=seq_s, layer=l)
        xs = _post(xs, o_a, o_b, o_c, mod_s, g_norm[l], goa, goc, w_out_b[l], w_gu_b[l], w_down_b[l])

    return (xp.reshape(nbp, seq_p, d), xs.reshape(nbs, seq_s, d),
            jnp.stack(ka_l, axis=1), jnp.stack(va_l, axis=1), jnp.stack(sb_l, axis=1),
            jnp.stack(kc_l, axis=1), jnp.stack(vc_l, axis=1))
```

```python
import functools

import numpy as np
import jax
import jax.numpy as jnp
from jax import lax
from jax.experimental import pallas as pl
from jax.experimental.pallas import tpu as pltpu

F32 = jnp.float32
BF16 = jnp.bfloat16

D_MODEL = 1024
HEAD_DIM = 64
GRID_W = 64
A_HEADS, A_KV_HEADS, B_HEADS, C_HEADS = 6, 2, 4, 6
A_WIDTH, A_KV_WIDTH = A_HEADS * HEAD_DIM, A_KV_HEADS * HEAD_DIM
B_WIDTH, C_WIDTH = B_HEADS * HEAD_DIM, C_HEADS * HEAD_DIM
N_DIR = 2
CHUNK = 64
WIN_R, WIN_C = 8, 16
ROPE_THETA = 10000.0
D_FF = 2816
N_MOD = 6
EPS = 1e-6
LANES = 128
ZA_W = A_WIDTH + 2 * A_KV_WIDTH
ZB_W = 3 * B_WIDTH + B_WIDTH + LANES
ZC_W = 3 * C_WIDTH
GATE_OFF = 4 * B_WIDTH
VMEM_LIMIT = 56 * 1024 * 1024
TOKEN_TILE = 256
Q_TILE = 256
DELTA_UNROLL = 4
NBR_G = 8
NBR_KROWS = 16
NEG_INF = float("-inf")


def _cparams(sem):
    return pltpu.CompilerParams(dimension_semantics=sem, vmem_limit_bytes=VMEM_LIMIT)


def _split2(x):
    hi = x.astype(BF16)
    lo = (x - hi.astype(F32)).astype(BF16)
    return hi, lo


def _split3(x):
    hi = x.astype(BF16)
    r1 = x - hi.astype(F32)
    mid = r1.astype(BF16)
    lo = (r1 - mid.astype(F32)).astype(BF16)
    return hi, mid, lo


def _dot(a, b):
    return jnp.dot(a, b, preferred_element_type=F32)


def _nt(a, b):
    return lax.dot_general(a, b, (((1,), (1,)), ((), ())), preferred_element_type=F32)


def _tn(a, b):
    return lax.dot_general(a, b, (((0,), (0,)), ((), ())), preferred_element_type=F32)


def _head_ones(width):
    r = lax.broadcasted_iota(jnp.int32, (width, width), 0) // HEAD_DIM
    c = lax.broadcasted_iota(jnp.int32, (width, width), 1) // HEAD_DIM
    return jnp.where(r == c, 1.0, 0.0).astype(BF16)


def _head_sum(x, ones):
    hi, lo = _split2(x)
    return _dot(hi, ones) + _dot(lo, ones)


def _head_rmsnorm(x, g_row, ones):
    ms = _head_sum(x * x, ones) * (1.0 / HEAD_DIM)
    return x * lax.rsqrt(ms + EPS) * g_row


def _rmsnorm(x, g_row):
    ms = jnp.mean(x * x, axis=-1, keepdims=True)
    return x * lax.rsqrt(ms + EPS) * g_row


def _silu(x):
    return x * jax.nn.sigmoid(x)


def _rope(x, cos, sin_signed, width):
    reps = width // LANES
    if reps > 1:
        cos = jnp.concatenate([cos] * reps, axis=1)
        sin_signed = jnp.concatenate([sin_signed] * reps, axis=1)
    lane = lax.broadcasted_iota(jnp.int32, x.shape, 1)
    first = (lane & (HEAD_DIM // 2)) == 0
    partner = jnp.where(first, pltpu.roll(x, width - HEAD_DIM // 2, 1), pltpu.roll(x, HEAD_DIM // 2, 1))
    return x * cos + partner * sin_signed


def _mod_kernel(c_ref, w_ref, b_ref, o_ref):
    a = _silu(c_ref[...]).astype(BF16)
    o_ref[0] = _dot(a, w_ref[0].astype(BF16)) + b_ref[0]


def _modulation(cvec8, w_mod, b_mod):
    depth, d, n = w_mod.shape
    tn = 1536
    return pl.pallas_call(
        _mod_kernel,
        out_shape=jax.ShapeDtypeStruct((depth, 8, n), F32),
        grid=(depth, n // tn),
        in_specs=[pl.BlockSpec((8, d), lambda l, j: (0, 0)),
                  pl.BlockSpec((1, d, tn), lambda l, j: (l, 0, j)),
                  pl.BlockSpec((1, 1, tn), lambda l, j: (l, 0, j))],
        out_specs=pl.BlockSpec((1, 8, tn), lambda l, j: (l, 0, j)),
        compiler_params=_cparams(("arbitrary", "arbitrary")),
        name="modulation",
    )(cvec8, w_mod, b_mod.reshape(depth, 1, n))


def _inproj_kernel(x_ref, mod_ref, g_ref, w_ref, za_ref, zb_ref, zc_ref):
    x = x_ref[...]
    h = _rmsnorm(x, g_ref[...]) * (1.0 + mod_ref[0, 1:2, :]) + mod_ref[0, 0:1, :]
    z = _dot(h.astype(BF16), w_ref[...])
    za_ref[...] = z[:, 0:ZA_W]
    zb_ref[...] = z[:, ZA_W:ZA_W + ZB_W]
    zc_ref[...] = z[:, ZA_W + ZB_W:]


def _inproj(x, mod, g_row, w):
    m, d = x.shape
    tm = TOKEN_TILE
    nb = mod.shape[0]
    per = (m // nb) // tm
    n = w.shape[1]
    return pl.pallas_call(
        _inproj_kernel,
        out_shape=(jax.ShapeDtypeStruct((m, ZA_W), F32),
                   jax.ShapeDtypeStruct((m, ZB_W), F32),
                   jax.ShapeDtypeStruct((m, ZC_W), F32)),
        grid=(m // tm,),
        in_specs=[pl.BlockSpec((tm, d), lambda i: (i, 0)),
                  pl.BlockSpec((1, N_MOD, d), lambda i: (i // per, 0, 0)),
                  pl.BlockSpec((1, d), lambda i: (0, 0)),
                  pl.BlockSpec((d, n), lambda i: (0, 0), pipeline_mode=pl.Buffered(1))],
        out_specs=(pl.BlockSpec((tm, ZA_W), lambda i: (i, 0)),
                   pl.BlockSpec((tm, ZB_W), lambda i: (i, 0)),
                   pl.BlockSpec((tm, ZC_W), lambda i: (i, 0))),
        compiler_params=_cparams(("arbitrary",)),
        name="inproj",
    )(x, mod, g_row, w)


def _attn_kernel(*refs, n_q, n_kv, tq, l_self, l_ctx, norm, rope, emit_k):
    it = iter(refs)
    q_ref, k_ref, v_ref = next(it), next(it), next(it)
    kc_ref = vc_ref = gq_ref = gk_ref = cos_ref = sin_ref = kout_ref = None
    if l_ctx:
        kc_ref, vc_ref = next(it), next(it)
    if norm:
        gq_ref, gk_ref = next(it), next(it)
    if rope:
        cos_ref, sin_ref = next(it), next(it)
    o_ref = next(it)
    if emit_k:
        kout_ref = next(it)
    kbuf, vbuf = next(it), next(it)

    j = pl.program_id(1)
    wq, wk = n_q * HEAD_DIM, n_kv * HEAD_DIM
    grp = n_q // n_kv
    rb = 256

    @pl.when(j == 0)
    def _():
        ones_k = _head_ones(wk) if norm else None

        def body(i, carry):
            r0 = pl.multiple_of(i * rb, rb)
            k = k_ref[pl.ds(r0, rb), :]
            if norm:
                k = _head_rmsnorm(k, gk_ref[...], ones_k)
            if rope:
                k = _rope(k, cos_ref[pl.ds(r0, rb), :], sin_ref[pl.ds(r0, rb), :], wk)
            if emit_k:
                kout_ref[pl.ds(r0, rb), :] = k
            kbuf[pl.ds(r0, rb), :] = k.astype(BF16)
            vbuf[pl.ds(r0, rb), :] = v_ref[pl.ds(r0, rb), :].astype(BF16)
            return carry
        lax.fori_loop(0, l_self // rb, body, 0)
        if l_ctx:
            kbuf[l_self:l_self + l_ctx, :] = kc_ref[...].astype(BF16)
            vbuf[l_self:l_self + l_ctx, :] = vc_ref[...].astype(BF16)

    q = q_ref[...]
    if norm:
        q = _head_rmsnorm(q, gq_ref[...], _head_ones(wq))
    if rope:
        r0 = pl.multiple_of(j * tq, tq)
        q = _rope(q, cos_ref[pl.ds(r0, tq), :], sin_ref[pl.ds(r0, tq), :], wq)
    q = (q * (HEAD_DIM ** -0.5)).astype(BF16)
    for h in range(n_q):
        kv = h // grp
        qh = q[:, h * HEAD_DIM:(h + 1) * HEAD_DIM]
        s = _nt(qh, kbuf[:, kv * HEAD_DIM:(kv + 1) * HEAD_DIM])
        m = jnp.max(s, axis=-1, keepdims=True)
        p = jnp.exp(s - m)
        l = jnp.sum(p, axis=-1, keepdims=True)
        o = _dot(p.astype(BF16), vbuf[:, kv * HEAD_DIM:(kv + 1) * HEAD_DIM])
        o_ref[:, h * HEAD_DIM:(h + 1) * HEAD_DIM] = o / l


def _attention(z, *, nb, seq, n_q, n_kv, tq, layer=None, kc=None, vc=None, gq=None, gk=None,
               cos=None, sin=None, emit_k=False):
    wq, wk = n_q * HEAD_DIM, n_kv * HEAD_DIM
    nq = seq // tq
    l_ctx = 0 if kc is None else kc.shape[2]
    norm, rope = gq is not None, cos is not None
    kblk, vblk = wq // wk, wq // wk + 1
    args = [z, z, z]
    in_specs = [pl.BlockSpec((tq, wq), lambda b, j: (b * nq + j, 0)),
                pl.BlockSpec((seq, wk), lambda b, j: (b, kblk)),
                pl.BlockSpec((seq, wk), lambda b, j: (b, vblk))]
    if l_ctx:
        args += [kc, vc]
        in_specs += [pl.BlockSpec((None, None, l_ctx, wk), lambda b, j: (b, layer, 0, 0))] * 2
    if norm:
        args += [gq, gk]
        in_specs += [pl.BlockSpec((1, wq), lambda b, j: (0, 0)), pl.BlockSpec((1, wk), lambda b, j: (0, 0))]
    if rope:
        args += [cos, sin]
        in_specs += [pl.BlockSpec((seq, LANES), lambda b, j: (0, 0))] * 2
    out_shape = [jax.ShapeDtypeStruct((nb * seq, wq), F32)]
    out_specs = [pl.BlockSpec((tq, wq), lambda b, j: (b * nq + j, 0))]
    if emit_k:
        out_shape.append(jax.ShapeDtypeStruct((nb * seq, wk), F32))
        out_specs.append(pl.BlockSpec((seq, wk), lambda b, j: (b, 0)))
    kern = functools.partial(_attn_kernel, n_q=n_q, n_kv=n_kv, tq=tq, l_self=seq, l_ctx=l_ctx,
                             norm=norm, rope=rope, emit_k=emit_k)
    return pl.pallas_call(
        kern,
        out_shape=tuple(out_shape),
        grid=(nb, nq),
        in_specs=in_specs,
        out_specs=tuple(out_specs),
        scratch_shapes=[pltpu.VMEM((seq + l_ctx, wk), BF16), pltpu.VMEM((seq + l_ctx, wk), BF16)],
        compiler_params=_cparams(("arbitrary", "arbitrary")),
        name="attention",
    )(*args)


def _nbr_plan(rows):
    nblk = rows // NBR_G
    kstart, pairs, ent = [], {}, []
    for blk in range(nblk):
        r0 = blk * NBR_G
        ks = min(max(r0 - WIN_R // 2, 0), rows - NBR_KROWS)
        kstart.append(ks)
        for i in range(NBR_G):
            r = r0 + i
            rs = min(max(r - WIN_R // 2, 0), rows - WIN_R)
            assert ks <= rs and rs + WIN_R <= ks + NBR_KROWS
            for p in range(NBR_KROWS // 2):
                ab = []
                for krow in (ks + 2 * p, ks + 2 * p + 1):
                    valid = rs <= krow < rs + WIN_R
                    ab.append(krow - r + WIN_R if valid else 0)
                ent.append(pairs.setdefault(tuple(ab), len(pairs)))
    pair_list = sorted(pairs, key=pairs.get)
    return (np.asarray(kstart, np.int32), np.asarray(ent, np.int32),
            np.asarray([a for a, _ in pair_list], np.int32), np.asarray([b for _, b in pair_list], np.int32))


def _nbr_bias_tables(rpb, rows):
    _, _, left, right = _nbr_plan(rows)
    col = np.arange(GRID_W)
    c_start = np.clip(col - WIN_C // 2, 0, GRID_W - WIN_C)
    cmask = (col[None, :] >= c_start[:, None]) & (col[None, :] < c_start[:, None] + WIN_C)
    dc = np.clip(col[None, :] - col[:, None] + (WIN_C - 1), 0, 2 * WIN_C - 2)
    t = jnp.take(rpb.astype(F32), jnp.asarray(dc.reshape(-1)), axis=-1)
    t = t.reshape(rpb.shape[:3] + (GRID_W, GRID_W))
    t = jnp.where(jnp.asarray(cmask), t, NEG_INF)
    masked = jnp.full(t.shape[:2] + (1, GRID_W, GRID_W), NEG_INF, F32)
    tpad = jnp.concatenate([masked, t], axis=2)
    return jnp.concatenate([jnp.take(tpad, jnp.asarray(left), axis=2),
                            jnp.take(tpad, jnp.asarray(right), axis=2)], axis=-1)


def _nbr_kernel(ks_ref, ent_ref, q_ref, k_ref, v_ref, kc_ref, vc_ref, tt_ref, o_ref,
                kbuf, vbuf, kcbuf, vcbuf, s_ref, *, seq):
    blk = pl.program_id(1)
    nk = NBR_KROWS * GRID_W
    npair = NBR_KROWS // 2
    rb = 256

    @pl.when(blk == 0)
    def _():
        def body(i, carry):
            r0 = pl.multiple_of(i * rb, rb)
            kbuf[pl.ds(r0, rb), :] = k_ref[pl.ds(r0, rb), :].astype(BF16)
            vbuf[pl.ds(r0, rb), :] = v_ref[pl.ds(r0, rb), :].astype(BF16)
            return carry
        lax.fori_loop(0, seq // rb, body, 0)
        kcbuf[...] = kc_ref[...].astype(BF16)
        vcbuf[...] = vc_ref[...].astype(BF16)

    ks = pl.multiple_of(ks_ref[blk] * GRID_W, GRID_W)
    q = (q_ref[...] * (HEAD_DIM ** -0.5)).astype(BF16)
    for h in range(C_HEADS):
        hs = slice(h * HEAD_DIM, (h + 1) * HEAD_DIM)
        qh = q[:, hs]
        s_ref[...] = _nt(qh, kbuf[pl.ds(ks, nk), hs])
        for i in range(NBR_G):
            for p in range(npair):
                e = ent_ref[(blk * NBR_G + i) * npair + p]
                rs_, cs_ = slice(i * GRID_W, (i + 1) * GRID_W), slice(p * LANES, (p + 1) * LANES)
                s_ref[rs_, cs_] = s_ref[rs_, cs_] + tt_ref[h, e]
        s_loc = s_ref[...]
        s_ctx = _nt(qh, kcbuf[:, hs])
        m = jnp.maximum(jnp.max(s_loc, axis=-1, keepdims=True), jnp.max(s_ctx, axis=-1, keepdims=True))
        p_loc = jnp.exp(s_loc - m)
        p_ctx = jnp.exp(s_ctx - m)
        l = jnp.sum(p_loc, axis=-1, keepdims=True) + jnp.sum(p_ctx, axis=-1, keepdims=True)
        o = _dot(p_loc.astype(BF16), vbuf[pl.ds(ks, nk), hs]) + _dot(p_ctx.astype(BF16), vcbuf[:, hs])
        o_ref[:, hs] = o / l


def _nbr_attention(zc, kc, vc, tt, *, nb, seq, layer):
    rows = seq // GRID_W
    kstart, ent, _, _ = _nbr_plan(rows)
    nblk = rows // NBR_G
    tq = NBR_G * GRID_W
    nk = NBR_KROWS * GRID_W
    l_ctx = kc.shape[2]
    n_ent = tt.shape[2]
    smem = pl.BlockSpec(memory_space=pltpu.SMEM)
    return pl.pallas_call(
        functools.partial(_nbr_kernel, seq=seq),
        out_shape=jax.ShapeDtypeStruct((nb * seq, C_WIDTH), F32),
        grid=(nb, nblk),
        in_specs=[smem, smem,
                  pl.BlockSpec((tq, C_WIDTH), lambda b, j: (b * nblk + j, 0)),
                  pl.BlockSpec((seq, C_WIDTH), lambda b, j: (b, 1)),
                  pl.BlockSpec((seq, C_WIDTH), lambda b, j: (b, 2)),
                  pl.BlockSpec((None, None, l_ctx, C_WIDTH), lambda b, j: (b, layer, 0, 0)),
                  pl.BlockSpec((None, None, l_ctx, C_WIDTH), lambda b, j: (b, layer, 0, 0)),
                  pl.BlockSpec((None, C_HEADS, n_ent, GRID_W, LANES), lambda b, j: (layer, 0, 0, 0, 0))],
        out_specs=pl.BlockSpec((tq, C_WIDTH), lambda b, j: (b * nblk + j, 0)),
        scratch_shapes=[pltpu.VMEM((seq, C_WIDTH), BF16), pltpu.VMEM((seq, C_WIDTH), BF16),
                        pltpu.VMEM((l_ctx, C_WIDTH), BF16), pltpu.VMEM((l_ctx, C_WIDTH), BF16),
                        pltpu.VMEM((tq, nk), F32)],
        compiler_params=_cparams(("arbitrary", "arbitrary")),
        name="nbr_attention",
    )(jnp.asarray(kstart), jnp.asarray(ent), zc, zc, zc, kc, vc, tt)


def _bd(x, mask):
    return jnp.where(mask, jnp.concatenate([x] * B_HEADS, axis=0), jnp.zeros((), x.dtype))


def _unpack_bd(full, lane_head):
    out = None
    for g in range(B_HEADS):
        blk = jnp.where(lane_head == g, full[g * HEAD_DIM:(g + 1) * HEAD_DIM, :], 0.0)
        out = blk if out is None else out + blk
    return out


def _pk_mm(a, b, mask):
    return _dot(a.astype(BF16), _bd(b.astype(BF16), mask))


def _pk_unit_tri_inverses(lmats, lowers, ri, cj, mask):
    eye = jnp.where(ri == cj, 1.0, 0.0).astype(F32)
    ts = [eye] * len(lmats)
    s = 1
    while s < CHUNK:
        same_pair = ((ri ^ cj) & ~(2 * s - 1)) == 0
        r_hi, c_hi = (ri & s) != 0, (cj & s) != 0
        sel = {True: same_pair & r_hi & ~c_hi, False: same_pair & ~r_hi & c_hi}
        offs = [jnp.where(sel[lo], lm, 0.0) for lm, lo in zip(lmats, lowers)]
        if s == 1:
            ts = [t - off for t, off in zip(ts, offs)]
        else:
            mids = [_pk_mm(t, off, mask) for t, off in zip(ts, offs)]
            ts = [t - _pk_mm(m, t, mask) for t, m in zip(ts, mids)]
        s *= 2
    return ts


def _delta_kernel(*refs, seq, has_s0):
    it = iter(refs)
    zb_ref = next(it)
    s0_ref = next(it) if has_s0 else None
    convw_ref, alog_ref, dtb_ref, gon_ref = next(it), next(it), next(it), next(it)
    o_ref, sfin_ref = next(it), next(it)
    qt_s, au_s, p_s, n_s, gam_s, of_s, ob_s, st_s = (next(it) for _ in range(8))

    n = seq // CHUNK
    c64 = CHUNK
    pw = B_WIDTH
    row = lax.broadcasted_iota(jnp.int32, (c64, 1), 0)
    lane = lax.broadcasted_iota(jnp.int32, (c64, LANES), 1)
    ri = lax.broadcasted_iota(jnp.int32, (c64, pw), 0)
    lane_p = lax.broadcasted_iota(jnp.int32, (c64, pw), 1)
    cj = lane_p & (HEAD_DIM - 1)
    lane_head = lane_p // HEAD_DIM
    bd_mask = (lax.broadcasted_iota(jnp.int32, (pw, pw), 0) // HEAD_DIM
               == lax.broadcasted_iota(jnp.int32, (pw, pw), 1) // HEAD_DIM)
    ones_b = _head_ones(pw)
    w0, w1, w2 = convw_ref[0:1, :], convw_ref[1:2, :], convw_ref[2:3, :]
    neg_decay_rate = -jnp.exp(alog_ref[...])
    eye8 = jnp.where(lax.broadcasted_iota(jnp.int32, (8, LANES), 0) == lax.broadcasted_iota(jnp.int32, (8, LANES), 1),
                     1.0, 0.0).astype(BF16)
    expand = jnp.where(lax.broadcasted_iota(jnp.int32, (LANES, N_DIR * pw), 1) // HEAD_DIM
                       == lax.broadcasted_iota(jnp.int32, (LANES, N_DIR * pw), 0), 1.0, 0.0).astype(BF16)

    def chunk_terms(c):
        r0 = pl.multiple_of(c * c64, c64)
        x = zb_ref[pl.ds(r0, c64), 0:3 * B_WIDTH]
        prev = zb_ref[pl.ds(jnp.maximum(r0 - 1, 0), 1), 0:3 * B_WIDTH] * jnp.where(c > 0, 1.0, 0.0)
        nxt = zb_ref[pl.ds(jnp.minimum(r0 + c64, seq - 1), 1), 0:3 * B_WIDTH] * jnp.where(c < n - 1, 1.0, 0.0)
        x_up = jnp.where(row == 0, prev, pltpu.roll(x, 1, 0))
        x_dn = jnp.where(row == c64 - 1, nxt, pltpu.roll(x, c64 - 1, 0))
        y = _silu(x_up * w0 + x * w1 + x_dn * w2)
        q, k, v = y[:, 0:pw], y[:, pw:2 * pw], y[:, 2 * pw:]
        q = q * lax.rsqrt(_head_sum(q * q, ones_b) + EPS) * (HEAD_DIM ** -0.5)
        k = k * lax.rsqrt(_head_sum(k * k, ones_b) + EPS)

        ga = zb_ref[pl.ds(r0, c64), GATE_OFF:GATE_OFF + LANES]
        beta = jax.nn.sigmoid(ga)
        a = pltpu.roll(ga, LANES - N_DIR * B_HEADS, 1) + dtb_ref[...]
        g = neg_decay_rate * (jnp.maximum(a, 0.0) + jnp.log(1.0 + jnp.exp(-jnp.abs(a))))
        cf, cr = g, g
        sh = 1
        while sh < c64:
            cf = cf + jnp.where(row >= sh, pltpu.roll(cf, sh, 0), 0.0)
            cr = cr + jnp.where(row < c64 - sh, pltpu.roll(cr, c64 - sh, 0), 0.0)
            sh *= 2
        gc = jnp.where(lane < B_HEADS, cf, cr)
        g3 = _split3(gc)
        gct = _nt(eye8, g3[0]) + (_nt(eye8, g3[1]) + _nt(eye8, g3[2]))
        b2 = _split2(beta)
        ex = _dot(jnp.concatenate([b2[0], b2[1], g3[0], g3[1], g3[2]], axis=0), expand)
        bexp = ex[0:c64] + ex[c64:2 * c64]
        gexp = ex[2 * c64:3 * c64] + (ex[3 * c64:4 * c64] + ex[4 * c64:])

        k_bd = _bd(k.astype(BF16), bd_mask)
        q_bf = q.astype(BF16)
        probs = []
        for d in range(N_DIR):
            ls = slice(d * pw, (d + 1) * pw)
            bx, gx = bexp[:, ls], gexp[:, ls]
            incl = (ri >= cj) if d == 0 else (ri <= cj)
            strict = (ri > cj) if d == 0 else (ri < cj)
            last = c64 - 1 if d == 0 else 0
            g_row = jnp.concatenate([gct[d * B_HEADS + h:d * B_HEADS + h + 1, :] for h in range(B_HEADS)], axis=1)
            g_last = gx[last:last + 1, :]
            decay = jnp.exp(jnp.where(incl, gx - g_row, NEG_INF))
            kb = k * bx
            aa = _nt(jnp.concatenate([kb.astype(BF16), q_bf], axis=0), k_bd)
            eg = jnp.exp(gx)
            probs.append(dict(
                c=c, d=d, lmat=jnp.where(strict, aa[0:c64] * decay, 0.0),
                a_intra=jnp.where(incl, aa[c64:] * decay, 0.0).astype(BF16),
                vb=v * bx, kbe=kb * eg, qe=q * eg, k_dec=(k * jnp.exp(g_last - gx)).astype(BF16),
                gam=jnp.exp(g_last)))
        return probs

    def terms_loop(i, carry):
        probs = [p for j in range(DELTA_UNROLL) for p in chunk_terms(i * DELTA_UNROLL + j)]
        tinvs = _pk_unit_tri_inverses([p["lmat"] for p in probs], [p["d"] == 0 for p in probs], ri, cj, bd_mask)
        us = [_pk_mm(t, p["vb"], bd_mask).astype(BF16) for t, p in zip(tinvs, probs)]
        ws = [_pk_mm(t, p["kbe"], bd_mask).astype(BF16) for t, p in zip(tinvs, probs)]
        for p, u, w in zip(probs, us, ws):
            c, d = p["c"], p["d"]
            qt_s[c, d] = (p["qe"] - _dot(p["a_intra"], _bd(w, bd_mask))).astype(BF16)
            au_s[c, d] = _dot(p["a_intra"], _bd(u, bd_mask))
            full = _tn(p["k_dec"], jnp.concatenate([w, u], axis=1))
            p_s[c, d] = _unpack_bd(full[:, 0:pw], lane_head).astype(BF16)
            n_s[c, d] = _unpack_bd(full[:, pw:], lane_head)
            gam_s[c, d] = jnp.broadcast_to(p["gam"], (8, pw))
        return carry

    lax.fori_loop(0, n // DELTA_UNROLL, terms_loop, 0)

    for d in range(N_DIR):
        if has_s0:
            s_pk = jnp.concatenate([s0_ref[d, h] for h in range(B_HEADS)], axis=1)
            st_s[d] = _bd(s_pk, bd_mask)
        else:
            st_s[d] = jnp.zeros((pw, pw), F32)

    def scan(t, carry):
        for d in range(N_DIR):
            c = t if d == 0 else n - 1 - t
            r0 = pl.multiple_of(c * c64, c64)
            s = st_s[d]
            s_bf = s.astype(BF16)
            out_s = of_s if d == 0 else ob_s
            out_s[pl.ds(r0, c64), :] = _dot(qt_s[c, d], s_bf) + au_s[c, d]
            st_s[d] = (s * gam_s[c, d, 0:1, :] - _dot(_bd(p_s[c, d], bd_mask), s_bf)) + _bd(n_s[c, d], bd_mask)
        return carry

    lax.fori_loop(0, n, scan, 0)

    def finish(c, carry):
        r0 = pl.multiple_of(c * c64, c64)
        o = of_s[pl.ds(r0, c64), :] + ob_s[pl.ds(r0, c64), :]
        y = _head_rmsnorm(o, gon_ref[...], ones_b)
        o_ref[pl.ds(r0, c64), :] = y * _silu(zb_ref[pl.ds(r0, c64), 3 * B_WIDTH:4 * B_WIDTH])
        return carry

    lax.fori_loop(0, n, finish, 0)
    for d in range(N_DIR):
        s_pk = _unpack_bd(st_s[d], lax.broadcasted_iota(jnp.int32, (HEAD_DIM, pw), 1) // HEAD_DIM)
        for h in range(B_HEADS):
            sfin_ref[d, h] = s_pk[:, h * HEAD_DIM:(h + 1) * HEAD_DIM]


def _deltanet(zb, s0, conv_w, a_log, dt_bias, g_onorm, *, nb, seq, layer=None):
    n = seq // CHUNK
    has_s0 = s0 is not None
    pad = LANES - N_DIR * B_HEADS
    alog_row = jnp.pad(a_log.reshape(1, -1).astype(F32), ((0, 0), (0, pad)))
    dtb_row = jnp.pad(dt_bias.reshape(1, -1).astype(F32), ((0, 0), (0, pad)))
    gon_row = jnp.tile(g_onorm.astype(F32), B_HEADS).reshape(1, B_WIDTH)
    st_shape = (N_DIR, B_HEADS, HEAD_DIM, HEAD_DIM)
    args = [zb]
    in_specs = [pl.BlockSpec((seq, ZB_W), lambda b: (b, 0))]
    if has_s0:
        args.append(s0)
        in_specs.append(pl.BlockSpec((None, None) + st_shape, lambda b: (b, layer, 0, 0, 0, 0)))
    args += [conv_w.astype(F32), alog_row, dtb_row, gon_row]
    in_specs += [pl.BlockSpec((3, 3 * B_WIDTH), lambda b: (0, 0)),
                 pl.BlockSpec((1, LANES), lambda b: (0, 0)),
                 pl.BlockSpec((1, LANES), lambda b: (0, 0)),
                 pl.BlockSpec((1, B_WIDTH), lambda b: (0, 0))]
    per_chunk = (n, N_DIR, CHUNK, B_WIDTH)
    return pl.pallas_call(
        functools.partial(_delta_kernel, seq=seq, has_s0=has_s0),
        out_shape=(jax.ShapeDtypeStruct((nb * seq, B_WIDTH), F32),
                   jax.ShapeDtypeStruct((nb,) + st_shape, F32)),
        grid=(nb,),
        in_specs=in_specs,
        out_specs=(pl.BlockSpec((seq, B_WIDTH), lambda b: (b, 0)),
                   pl.BlockSpec((None,) + st_shape, lambda b: (b, 0, 0, 0, 0))),
        scratch_shapes=[pltpu.VMEM(per_chunk, BF16), pltpu.VMEM(per_chunk, F32),
                        pltpu.VMEM(per_chunk, BF16), pltpu.VMEM(per_chunk, F32),
                        pltpu.VMEM((n, N_DIR, 8, B_WIDTH), F32),
                        pltpu.VMEM((seq, B_WIDTH), F32), pltpu.VMEM((seq, B_WIDTH), F32),
                        pltpu.VMEM((N_DIR, B_WIDTH, B_WIDTH), F32)],
        compiler_params=_cparams(("arbitrary",)),
        name="deltanet",
    )(*args)


def _post_kernel(x_ref, oa_ref, ob_ref, oc_ref, mod_ref, gn_ref, ga_ref, gc_ref,
                 wo_ref, wgu_ref, wd_ref, y_ref):
    x = x_ref[...]
    ya = _rmsnorm(oa_ref[...], ga_ref[...])
    yc = _rmsnorm(oc_ref[...], gc_ref[...])
    y = jnp.concatenate([ya, ob_ref[...], yc], axis=-1).astype(BF16)
    m = _dot(y, wo_ref[...])
    x1 = x + mod_ref[0, 2:3, :] * _rmsnorm(m, gn_ref[1:2, :])
    h = _rmsnorm(x1, gn_ref[2:3, :]) * (1.0 + mod_ref[0, 4:5, :]) + mod_ref[0, 3:4, :]
    gu = _dot(h.astype(BF16), wgu_ref[...])
    act = (_silu(gu[:, 0:D_FF]) * gu[:, D_FF:]).astype(BF16)
    f = _dot(act, wd_ref[...])
    y_ref[...] = x1 + mod_ref[0, 5:6, :] * _rmsnorm(f, gn_ref[3:4, :])


def _post(x, oa, ob, oc, mod, g_norm, g_out_a, g_out_c, w_out, w_gu, w_down):
    m, d = x.shape
    tm = TOKEN_TILE
    nb = mod.shape[0]
    per = (m // nb) // tm
    row = lambda i: (i, 0)
    const = lambda i: (0, 0)
    single = pl.Buffered(1)
    return pl.pallas_call(
        _post_kernel,
        out_shape=jax.ShapeDtypeStruct((m, d), F32),
        grid=(m // tm,),
        in_specs=[pl.BlockSpec((tm, d), row),
                  pl.BlockSpec((tm, A_WIDTH), row),
                  pl.BlockSpec((tm, B_WIDTH), row),
                  pl.BlockSpec((tm, C_WIDTH), row),
                  pl.BlockSpec((1, N_MOD, d), lambda i: (i // per, 0, 0)),
                  pl.BlockSpec((4, d), const),
                  pl.BlockSpec((1, A_WIDTH), const),
                  pl.BlockSpec((1, C_WIDTH), const),
                  pl.BlockSpec(w_out.shape, const, pipeline_mode=single),
                  pl.BlockSpec(w_gu.shape, const, pipeline_mode=single),
                  pl.BlockSpec(w_down.shape, const, pipeline_mode=single)],
        out_specs=pl.BlockSpec((tm, d), row),
        compiler_params=_cparams(("arbitrary",)),
        name="post",
    )(x, oa, ob, oc, mod, g_norm, g_out_a, g_out_c, w_out, w_gu, w_down)


def _rope_tables(seq):
    quarter = HEAD_DIM // 4
    inv = ROPE_THETA ** (-jnp.arange(quarter, dtype=F32) / quarter)
    t = jnp.arange(seq)
    rowp = (t // GRID_W).astype(F32)
    colp = (t % GRID_W).astype(F32)
    ang = jnp.concatenate([rowp[:, None] * inv, colp[:, None] * inv], axis=-1)
    cos, sin = jnp.cos(ang), jnp.sin(ang)
    reps = LANES // HEAD_DIM
    return (jnp.tile(jnp.concatenate([cos, cos], axis=-1), (1, reps)),
            jnp.tile(jnp.concatenate([-sin, sin], axis=-1), (1, reps)))


def _permute_w_in(w_in):
    cut = ZA_W + 4 * B_WIDTH + 2 * N_DIR * B_HEADS
    pad = jnp.zeros(w_in.shape[:-1] + (LANES - 2 * N_DIR * B_HEADS,), w_in.dtype)
    return jnp.concatenate([w_in[..., :cut], pad, w_in[..., cut:]], axis=-1).astype(BF16)


def kernel(x_prompt, x_sample, cache_a_k, cache_a_v, state_b, cache_c_k, cache_c_v, c, c_ctx, w_mod, b_mod,
           g_norm, w_in, g_qk_a, g_out_a, conv_w, a_log, dt_bias, g_onorm_b, rpb, g_out_c, w_out, w_gu, w_down):
    nbp, seq_p, d = x_prompt.shape
    nbs, seq_s, _ = x_sample.shape
    depth = w_mod.shape[0]
    past = cache_a_k.shape[2]

    cvec = jnp.concatenate([c_ctx[None, :], c, jnp.zeros((8 - 1 - nbs, d), F32)], axis=0)
    mods = _modulation(cvec, w_mod, b_mod)
    w_in_p = _permute_w_in(w_in)
    w_out_b, w_gu_b, w_down_b = w_out.astype(BF16), w_gu.astype(BF16), w_down.astype(BF16)
    cos, sin = _rope_tables(seq_s)
    tt = _nbr_bias_tables(rpb, seq_s // GRID_W)
    ck_a = cache_a_k.reshape(nbs, depth, past, A_KV_WIDTH)
    cv_a = cache_a_v.reshape(nbs, depth, past, A_KV_WIDTH)
    ck_c = cache_c_k.reshape(nbs, depth, past, C_WIDTH)
    cv_c = cache_c_v.reshape(nbs, depth, past, C_WIDTH)

    xp = x_prompt.reshape(nbp * seq_p, d)
    xs = x_sample.reshape(nbs * seq_s, d)
    ka_l, va_l, sb_l, kc_l, vc_l = [], [], [], [], []
    for l in range(depth):
        gq = jnp.tile(g_qk_a[l, 0], A_HEADS).reshape(1, A_WIDTH)
        gk = jnp.tile(g_qk_a[l, 1], A_KV_HEADS).reshape(1, A_KV_WIDTH)
        goa, goc = g_out_a[l].reshape(1, A_WIDTH), g_out_c[l].reshape(1, C_WIDTH)
        g0 = g_norm[l, 0:1]
        mod_p = mods[l, 0:1].reshape(1, N_MOD, d)
        mod_s = mods[l, 1:1 + nbs].reshape(nbs, N_MOD, d)

        za, zb, zc = _inproj(xp, mod_p, g0, w_in_p[l])
        o_a, ka = _attention(za, nb=nbp, seq=seq_p, n_q=A_HEADS, n_kv=A_KV_HEADS, tq=seq_p,
                             gq=gq, gk=gk, emit_k=True)
        o_b, s_b = _deltanet(zb, None, conv_w[l], a_log[l], dt_bias[l], g_onorm_b[l], nb=nbp, seq=seq_p)
        (o_c,) = _attention(zc, nb=nbp, seq=seq_p, n_q=C_HEADS, n_kv=C_HEADS, tq=seq_p)
        xp = _post(xp, o_a, o_b, o_c, mod_p, g_norm[l], goa, goc, w_out_b[l], w_gu_b[l], w_down_b[l])
        ka_l.append(ka.reshape(nbp, seq_p, A_KV_HEADS, HEAD_DIM))
        va_l.append(za[:, A_WIDTH + A_KV_WIDTH:].reshape(nbp, seq_p, A_KV_HEADS, HEAD_DIM))
        sb_l.append(s_b)
        kc_l.append(zc[:, C_WIDTH:2 * C_WIDTH].reshape(nbp, seq_p, C_HEADS, HEAD_DIM))
        vc_l.append(zc[:, 2 * C_WIDTH:].reshape(nbp, seq_p, C_HEADS, HEAD_DIM))

        za, zb, zc = _inproj(xs, mod_s, g0, w_in_p[l])
        (o_a,) = _attention(za, nb=nbs, seq=seq_s, n_q=A_HEADS, n_kv=A_KV_HEADS, tq=Q_TILE, layer=l,
                            kc=ck_a, vc=cv_a, gq=gq, gk=gk, cos=cos, sin=sin)
        o_b, _ = _deltanet(zb, state_b, conv_w[l], a_log[l], dt_bias[l], g_onorm_b[l], nb=nbs, seq=seq_s, layer=l)
        o_c = _nbr_attention(zc, ck_c, cv_c, tt, nb=nbs, seq=seq_s, layer=l)
        xs = _post(xs, o_a, o_b, o_c, mod_s, g_norm[l], goa, goc, w_out_b[l], w_gu_b[l], w_down_b[l])

    return (xp.reshape(nbp, seq_p, d), xs.reshape(nbs, seq_s, d),
            jnp.stack(ka_l, axis=1), jnp.stack(va_l, axis=1), jnp.stack(sb_l, axis=1),
            jnp.stack(kc_l, axis=1), jnp.stack(vc_l, axis=1))
```

```python
import functools

import numpy as np
import jax
import jax.numpy as jnp
from jax import lax
from jax.experimental import pallas as pl
from jax.experimental.pallas import tpu as pltpu

F32 = jnp.float32
BF16 = jnp.bfloat16

D_MODEL = 1024
HEAD_DIM = 64
GRID_W = 64
A_HEADS, A_KV_HEADS, B_HEADS, C_HEADS = 6, 2, 4, 6
A_WIDTH, A_KV_WIDTH = A_HEADS * HEAD_DIM, A_KV_HEADS * HEAD_DIM
B_WIDTH, C_WIDTH = B_HEADS * HEAD_DIM, C_HEADS * HEAD_DIM
N_DIR = 2
CHUNK = 64
WIN_R, WIN_C = 8, 16
ROPE_THETA = 10000.0
D_FF = 2816
N_MOD = 6
EPS = 1e-6
LANES = 128
ZA_W = A_WIDTH + 2 * A_KV_WIDTH
ZB_W = 3 * B_WIDTH + B_WIDTH + LANES
ZC_W = 3 * C_WIDTH
GATE_OFF = 4 * B_WIDTH
VMEM_LIMIT = 56 * 1024 * 1024
TOKEN_TILE = 256
Q_TILE = 256
DELTA_UNROLL = 4
NBR_G = 4
NBR_KROWS = 12
NEG_INF = float("-inf")


def _cparams(sem):
    return pltpu.CompilerParams(dimension_semantics=sem, vmem_limit_bytes=VMEM_LIMIT)


def _split2(x):
    hi = x.astype(BF16)
    lo = (x - hi.astype(F32)).astype(BF16)
    return hi, lo


def _split3(x):
    hi = x.astype(BF16)
    r1 = x - hi.astype(F32)
    mid = r1.astype(BF16)
    lo = (r1 - mid.astype(F32)).astype(BF16)
    return hi, mid, lo


def _dot(a, b):
    return jnp.dot(a, b, preferred_element_type=F32)


def _nt(a, b):
    return lax.dot_general(a, b, (((1,), (1,)), ((), ())), preferred_element_type=F32)


def _tn(a, b):
    return lax.dot_general(a, b, (((0,), (0,)), ((), ())), preferred_element_type=F32)


def _head_ones(width):
    r = lax.broadcasted_iota(jnp.int32, (width, width), 0) // HEAD_DIM
    c = lax.broadcasted_iota(jnp.int32, (width, width), 1) // HEAD_DIM
    return jnp.where(r == c, 1.0, 0.0).astype(BF16)


def _head_sum(x, ones):
    hi, lo = _split2(x)
    return _dot(hi, ones) + _dot(lo, ones)


def _head_rmsnorm(x, g_row, ones):
    ms = _head_sum(x * x, ones) * (1.0 / HEAD_DIM)
    return x * lax.rsqrt(ms + EPS) * g_row


def _rmsnorm(x, g_row):
    ms = jnp.mean(x * x, axis=-1, keepdims=True)
    return x * lax.rsqrt(ms + EPS) * g_row


def _silu(x):
    return x * jax.nn.sigmoid(x)


def _rope(x, cos, sin_signed, width):
    reps = width // LANES
    if reps > 1:
        cos = jnp.concatenate([cos] * reps, axis=1)
        sin_signed = jnp.concatenate([sin_signed] * reps, axis=1)
    lane = lax.broadcasted_iota(jnp.int32, x.shape, 1)
    first = (lane & (HEAD_DIM // 2)) == 0
    partner = jnp.where(first, pltpu.roll(x, width - HEAD_DIM // 2, 1), pltpu.roll(x, HEAD_DIM // 2, 1))
    return x * cos + partner * sin_signed


def _mod_kernel(c_ref, w_ref, b_ref, o_ref):
    a = _silu(c_ref[...]).astype(BF16)
    o_ref[0] = _dot(a, w_ref[0].astype(BF16)) + b_ref[0]


def _modulation(cvec8, w_mod, b_mod):
    depth, d, n = w_mod.shape
    tn = 1536
    return pl.pallas_call(
        _mod_kernel,
        out_shape=jax.ShapeDtypeStruct((depth, 8, n), F32),
        grid=(depth, n // tn),
        in_specs=[pl.BlockSpec((8, d), lambda l, j: (0, 0)),
                  pl.BlockSpec((1, d, tn), lambda l, j: (l, 0, j)),
                  pl.BlockSpec((1, 1, tn), lambda l, j: (l, 0, j))],
        out_specs=pl.BlockSpec((1, 8, tn), lambda l, j: (l, 0, j)),
        compiler_params=_cparams(("arbitrary", "arbitrary")),
        name="modulation",
    )(cvec8, w_mod, b_mod.reshape(depth, 1, n))


INPROJ_OUT_W = (A_WIDTH, A_KV_WIDTH, A_KV_WIDTH, ZB_W, C_WIDTH, C_WIDTH, C_WIDTH)


def _inproj_kernel(x_ref, mod_ref, g_ref, wab_ref, wg_ref, wc_ref,
                   aq_ref, ak_ref, av_ref, zb_ref, cq_ref, ck_ref, cv_ref):
    x = x_ref[...]
    h = (_rmsnorm(x, g_ref[...]) * (1.0 + mod_ref[0, 1:2, :]) + mod_ref[0, 0:1, :]).astype(BF16)
    z = _dot(h, wab_ref[...])
    aq_ref[...] = z[:, 0:A_WIDTH]
    ak_ref[...] = z[:, A_WIDTH:A_WIDTH + A_KV_WIDTH]
    av_ref[...] = z[:, A_WIDTH + A_KV_WIDTH:ZA_W]
    zb_ref[:, 0:GATE_OFF] = z[:, ZA_W:]
    zb_ref[:, GATE_OFF:] = _dot(h, wg_ref[...])
    zc = _dot(h, wc_ref[...])
    cq_ref[...] = zc[:, 0:C_WIDTH]
    ck_ref[...] = zc[:, C_WIDTH:2 * C_WIDTH]
    cv_ref[...] = zc[:, 2 * C_WIDTH:]


def _inproj(x, mod, g_row, w_ab, w_g, w_c):
    m, d = x.shape
    tm = TOKEN_TILE
    nb = mod.shape[0]
    per = (m // nb) // tm
    const = lambda i: (0, 0)
    single = pl.Buffered(1)
    return pl.pallas_call(
        _inproj_kernel,
        out_shape=tuple(jax.ShapeDtypeStruct((m, w), F32) for w in INPROJ_OUT_W),
        grid=(m // tm,),
        in_specs=[pl.BlockSpec((tm, d), lambda i: (i, 0)),
                  pl.BlockSpec((1, N_MOD, d), lambda i: (i // per, 0, 0)),
                  pl.BlockSpec((1, d), const),
                  pl.BlockSpec(w_ab.shape, const, pipeline_mode=single),
                  pl.BlockSpec(w_g.shape, const, pipeline_mode=single),
                  pl.BlockSpec(w_c.shape, const, pipeline_mode=single)],
        out_specs=tuple(pl.BlockSpec((tm, w), lambda i: (i, 0)) for w in INPROJ_OUT_W),
        compiler_params=_cparams(("arbitrary",)),
        name="inproj",
    )(x, mod, g_row, w_ab, w_g, w_c)


def _attn_kernel(*refs, n_q, n_kv, tq, l_self, l_ctx, norm, rope, emit_k):
    it = iter(refs)
    q_ref, k_ref, v_ref = next(it), next(it), next(it)
    kc_ref = vc_ref = gq_ref = gk_ref = cos_ref = sin_ref = kout_ref = None
    if l_ctx:
        kc_ref, vc_ref = next(it), next(it)
    if norm:
        gq_ref, gk_ref = next(it), next(it)
    if rope:
        cos_ref, sin_ref = next(it), next(it)
    o_ref = next(it)
    if emit_k:
        kout_ref = next(it)
    kbuf, vbuf = next(it), next(it)

    j = pl.program_id(1)
    wq, wk = n_q * HEAD_DIM, n_kv * HEAD_DIM
    grp = n_q // n_kv
    rb = 256

    @pl.when(j == 0)
    def _():
        ones_k = _head_ones(wk) if norm else None

        def body(i, carry):
            r0 = pl.multiple_of(i * rb, rb)
            k = k_ref[pl.ds(r0, rb), :]
            if norm:
                k = _head_rmsnorm(k, gk_ref[...], ones_k)
            if rope:
                k = _rope(k, cos_ref[pl.ds(r0, rb), :], sin_ref[pl.ds(r0, rb), :], wk)
            if emit_k:
                kout_ref[pl.ds(r0, rb), :] = k
            kbuf[pl.ds(r0, rb), :] = k.astype(BF16)
            vbuf[pl.ds(r0, rb), :] = v_ref[pl.ds(r0, rb), :].astype(BF16)
            return carry
        lax.fori_loop(0, l_self // rb, body, 0)
        if l_ctx:
            kbuf[l_self:l_self + l_ctx, :] = kc_ref[...].astype(BF16)
            vbuf[l_self:l_self + l_ctx, :] = vc_ref[...].astype(BF16)

    q = q_ref[...]
    if norm:
        q = _head_rmsnorm(q, gq_ref[...], _head_ones(wq))
    if rope:
        r0 = pl.multiple_of(j * tq, tq)
        q = _rope(q, cos_ref[pl.ds(r0, tq), :], sin_ref[pl.ds(r0, tq), :], wq)
    q = (q * (HEAD_DIM ** -0.5)).astype(BF16)
    for h in range(n_q):
        kv = h // grp
        qh = q[:, h * HEAD_DIM:(h + 1) * HEAD_DIM]
        s = _nt(qh, kbuf[:, kv * HEAD_DIM:(kv + 1) * HEAD_DIM])
        m = jnp.max(s, axis=-1, keepdims=True)
        p = jnp.exp(s - m)
        l = jnp.sum(p, axis=-1, keepdims=True)
        o = _dot(p.astype(BF16), vbuf[:, kv * HEAD_DIM:(kv + 1) * HEAD_DIM])
        o_ref[:, h * HEAD_DIM:(h + 1) * HEAD_DIM] = o / l


def _attention(q, k, v, *, nb, seq, n_q, n_kv, tq, layer=None, kc=None, vc=None, gq=None, gk=None,
               cos=None, sin=None, emit_k=False):
    wq, wk = n_q * HEAD_DIM, n_kv * HEAD_DIM
    nq = seq // tq
    l_ctx = 0 if kc is None else kc.shape[2]
    norm, rope = gq is not None, cos is not None
    args = [q, k, v]
    in_specs = [pl.BlockSpec((tq, wq), lambda b, j: (b * nq + j, 0)),
                pl.BlockSpec((seq, wk), lambda b, j: (b, 0)),
                pl.BlockSpec((seq, wk), lambda b, j: (b, 0))]
    if l_ctx:
        args += [kc, vc]
        in_specs += [pl.BlockSpec((None, None, l_ctx, wk), lambda b, j: (b, layer, 0, 0))] * 2
    if norm:
        args += [gq, gk]
        in_specs += [pl.BlockSpec((1, wq), lambda b, j: (0, 0)), pl.BlockSpec((1, wk), lambda b, j: (0, 0))]
    if rope:
        args += [cos, sin]
        in_specs += [pl.BlockSpec((seq, LANES), lambda b, j: (0, 0))] * 2
    out_shape = [jax.ShapeDtypeStruct((nb * seq, wq), F32)]
    out_specs = [pl.BlockSpec((tq, wq), lambda b, j: (b * nq + j, 0))]
    if emit_k:
        out_shape.append(jax.ShapeDtypeStruct((nb * seq, wk), F32))
        out_specs.append(pl.BlockSpec((seq, wk), lambda b, j: (b, 0)))
    kern = functools.partial(_attn_kernel, n_q=n_q, n_kv=n_kv, tq=tq, l_self=seq, l_ctx=l_ctx,
                             norm=norm, rope=rope, emit_k=emit_k)
    return pl.pallas_call(
        kern,
        out_shape=tuple(out_shape),
        grid=(nb, nq),
        in_specs=in_specs,
        out_specs=tuple(out_specs),
        scratch_shapes=[pltpu.VMEM((seq + l_ctx, wk), BF16), pltpu.VMEM((seq + l_ctx, wk), BF16)],
        compiler_params=_cparams(("arbitrary", "arbitrary")),
        name="attention",
    )(*args)


def _nbr_plan(rows):
    nblk = rows // NBR_G
    kstart, pairs, ent = [], {}, []
    for blk in range(nblk):
        r0 = blk * NBR_G
        ks = min(max(r0 - WIN_R // 2, 0), rows - NBR_KROWS)
        kstart.append(ks)
        for i in range(NBR_G):
            r = r0 + i
            rs = min(max(r - WIN_R // 2, 0), rows - WIN_R)
            assert ks <= rs and rs + WIN_R <= ks + NBR_KROWS
            for p in range(NBR_KROWS // 2):
                ab = []
                for krow in (ks + 2 * p, ks + 2 * p + 1):
                    valid = rs <= krow < rs + WIN_R
                    ab.append(krow - r + WIN_R if valid else 0)
                ent.append(pairs.setdefault(tuple(ab), len(pairs)))
    pair_list = sorted(pairs, key=pairs.get)
    return (np.asarray(kstart, np.int32), np.asarray(ent, np.int32),
            np.asarray([a for a, _ in pair_list], np.int32), np.asarray([b for _, b in pair_list], np.int32))


def _nbr_bias_tables(rpb, rows):
    _, _, left, right = _nbr_plan(rows)
    col = np.arange(GRID_W)
    c_start = np.clip(col - WIN_C // 2, 0, GRID_W - WIN_C)
    cmask = (col[None, :] >= c_start[:, None]) & (col[None, :] < c_start[:, None] + WIN_C)
    dc = np.clip(col[None, :] - col[:, None] + (WIN_C - 1), 0, 2 * WIN_C - 2)
    onehot = (dc.reshape(-1)[None, :] == np.arange(2 * WIN_C - 1)[:, None]).astype(np.float32)
    t = jnp.einsum("lhde,ek->lhdk", rpb.astype(F32), jnp.asarray(onehot), precision=lax.Precision.HIGHEST)
    t = t.reshape(rpb.shape[:3] + (GRID_W, GRID_W))
    t = jnp.where(jnp.asarray(cmask), t, NEG_INF)
    masked = jnp.full(t.shape[:2] + (1, GRID_W, GRID_W), NEG_INF, F32)
    tpad = jnp.concatenate([masked, t], axis=2)
    return jnp.concatenate([jnp.take(tpad, jnp.asarray(left), axis=2),
                            jnp.take(tpad, jnp.asarray(right), axis=2)], axis=-1)


def _nbr_kernel(ks_ref, ent_ref, q_ref, k_ref, v_ref, kc_ref, vc_ref, tt_ref, o_ref,
                kbuf, vbuf, kcbuf, vcbuf, s_ref, *, seq):
    blk = pl.program_id(1)
    nk = NBR_KROWS * GRID_W
    npair = NBR_KROWS // 2
    rb = 256

    @pl.when(blk == 0)
    def _():
        def body(i, carry):
            r0 = pl.multiple_of(i * rb, rb)
            kbuf[pl.ds(r0, rb), :] = k_ref[pl.ds(r0, rb), :].astype(BF16)
            vbuf[pl.ds(r0, rb), :] = v_ref[pl.ds(r0, rb), :].astype(BF16)
            return carry
        lax.fori_loop(0, seq // rb, body, 0)
        kcbuf[...] = kc_ref[...].astype(BF16)
        vcbuf[...] = vc_ref[...].astype(BF16)

    ks = pl.multiple_of(ks_ref[blk] * GRID_W, GRID_W)
    q = (q_ref[...] * (HEAD_DIM ** -0.5)).astype(BF16)
    for h in range(C_HEADS):
        hs = slice(h * HEAD_DIM, (h + 1) * HEAD_DIM)
        qh = q[:, hs]
        s_ref[...] = _nt(qh, kbuf[pl.ds(ks, nk), hs])
        for i in range(NBR_G):
            for p in range(npair):
                e = ent_ref[(blk * NBR_G + i) * npair + p]
                rs_, cs_ = slice(i * GRID_W, (i + 1) * GRID_W), slice(p * LANES, (p + 1) * LANES)
                s_ref[rs_, cs_] = s_ref[rs_, cs_] + tt_ref[h, e]
        s_loc = s_ref[...]
        s_ctx = _nt(qh, kcbuf[:, hs])
        m = jnp.maximum(jnp.max(s_loc, axis=-1, keepdims=True), jnp.max(s_ctx, axis=-1, keepdims=True))
        p_loc = jnp.exp(s_loc - m)
        p_ctx = jnp.exp(s_ctx - m)
        l = jnp.sum(p_loc, axis=-1, keepdims=True) + jnp.sum(p_ctx, axis=-1, keepdims=True)
        o = _dot(p_loc.astype(BF16), vbuf[pl.ds(ks, nk), hs]) + _dot(p_ctx.astype(BF16), vcbuf[:, hs])
        o_ref[:, hs] = o / l


def _nbr_attention(q, k, v, kc, vc, tt, *, nb, seq, layer):
    rows = seq // GRID_W
    kstart, ent, _, _ = _nbr_plan(rows)
    nblk = rows // NBR_G
    tq = NBR_G * GRID_W
    nk = NBR_KROWS * GRID_W
    l_ctx = kc.shape[2]
    n_ent = tt.shape[2]
    smem = pl.BlockSpec(memory_space=pltpu.SMEM)
    return pl.pallas_call(
        functools.partial(_nbr_kernel, seq=seq),
        out_shape=jax.ShapeDtypeStruct((nb * seq, C_WIDTH), F32),
        grid=(nb, nblk),
        in_specs=[smem, smem,
                  pl.BlockSpec((tq, C_WIDTH), lambda b, j: (b * nblk + j, 0)),
                  pl.BlockSpec((seq, C_WIDTH), lambda b, j: (b, 0)),
                  pl.BlockSpec((seq, C_WIDTH), lambda b, j: (b, 0)),
                  pl.BlockSpec((None, None, l_ctx, C_WIDTH), lambda b, j: (b, layer, 0, 0)),
                  pl.BlockSpec((None, None, l_ctx, C_WIDTH), lambda b, j: (b, layer, 0, 0)),
                  pl.BlockSpec((None, C_HEADS, n_ent, GRID_W, LANES), lambda b, j: (layer, 0, 0, 0, 0))],
        out_specs=pl.BlockSpec((tq, C_WIDTH), lambda b, j: (b * nblk + j, 0)),
        scratch_shapes=[pltpu.VMEM((seq, C_WIDTH), BF16), pltpu.VMEM((seq, C_WIDTH), BF16),
                        pltpu.VMEM((l_ctx, C_WIDTH), BF16), pltpu.VMEM((l_ctx, C_WIDTH), BF16),
                        pltpu.VMEM((tq, nk), F32)],
        compiler_params=_cparams(("arbitrary", "arbitrary")),
        name="nbr_attention",
    )(jnp.asarray(kstart), jnp.asarray(ent), q, k, v, kc, vc, tt)


def _bd(x, mask):
    return jnp.where(mask, jnp.concatenate([x] * B_HEADS, axis=0), jnp.zeros((), x.dtype))


def _unpack_bd(full, lane_head):
    out = None
    for g in range(B_HEADS):
        blk = jnp.where(lane_head == g, full[g * HEAD_DIM:(g + 1) * HEAD_DIM, :], 0.0)
        out = blk if out is None else out + blk
    return out


def _pk_mm(a, b, mask):
    return _dot(a.astype(BF16), _bd(b.astype(BF16), mask))


def _pk_unit_tri_inverses(lmats, lowers, ri, cj, mask):
    eye = jnp.where(ri == cj, 1.0, 0.0).astype(F32)
    ts = [eye] * len(lmats)
    s = 1
    while s < CHUNK:
        same_pair = ((ri ^ cj) & ~(2 * s - 1)) == 0
        r_hi, c_hi = (ri & s) != 0, (cj & s) != 0
        sel = {True: same_pair & r_hi & ~c_hi, False: same_pair & ~r_hi & c_hi}
        offs = [jnp.where(sel[lo], lm, 0.0) for lm, lo in zip(lmats, lowers)]
        if s == 1:
            ts = [t - off for t, off in zip(ts, offs)]
        else:
            mids = [_pk_mm(t, off, mask) for t, off in zip(ts, offs)]
            ts = [t - _pk_mm(m, t, mask) for t, m in zip(ts, mids)]
        s *= 2
    return ts


def _delta_kernel(*refs, seq, has_s0):
    it = iter(refs)
    zb_ref = next(it)
    s0_ref = next(it) if has_s0 else None
    convw_ref, alog_ref, dtb_ref, gon_ref = next(it), next(it), next(it), next(it)
    o_ref, sfin_ref = next(it), next(it)
    qt_s, au_s, p_s, n_s, gam_s, of_s, ob_s, st_s = (next(it) for _ in range(8))

    n = seq // CHUNK
    c64 = CHUNK
    pw = B_WIDTH
    row = lax.broadcasted_iota(jnp.int32, (c64, 1), 0)
    lane = lax.broadcasted_iota(jnp.int32, (c64, LANES), 1)
    ri = lax.broadcasted_iota(jnp.int32, (c64, pw), 0)
    lane_p = lax.broadcasted_iota(jnp.int32, (c64, pw), 1)
    cj = lane_p & (HEAD_DIM - 1)
    lane_head = lane_p // HEAD_DIM
    bd_mask = (lax.broadcasted_iota(jnp.int32, (pw, pw), 0) // HEAD_DIM
               == lax.broadcasted_iota(jnp.int32, (pw, pw), 1) // HEAD_DIM)
    ones_b = _head_ones(pw)
    w0, w1, w2 = convw_ref[0:1, :], convw_ref[1:2, :], convw_ref[2:3, :]
    neg_decay_rate = -jnp.exp(alog_ref[...])
    eye8 = jnp.where(lax.broadcasted_iota(jnp.int32, (8, LANES), 0) == lax.broadcasted_iota(jnp.int32, (8, LANES), 1),
                     1.0, 0.0).astype(BF16)
    expand = jnp.where(lax.broadcasted_iota(jnp.int32, (LANES, N_DIR * pw), 1) // HEAD_DIM
                       == lax.broadcasted_iota(jnp.int32, (LANES, N_DIR * pw), 0), 1.0, 0.0).astype(BF16)

    def chunk_terms(c):
        r0 = pl.multiple_of(c * c64, c64)
        x = zb_ref[pl.ds(r0, c64), 0:3 * B_WIDTH]
        prev = zb_ref[pl.ds(jnp.maximum(r0 - 1, 0), 1), 0:3 * B_WIDTH] * jnp.where(c > 0, 1.0, 0.0)
        nxt = zb_ref[pl.ds(jnp.minimum(r0 + c64, seq - 1), 1), 0:3 * B_WIDTH] * jnp.where(c < n - 1, 1.0, 0.0)
        x_up = jnp.where(row == 0, prev, pltpu.roll(x, 1, 0))
        x_dn = jnp.where(row == c64 - 1, nxt, pltpu.roll(x, c64 - 1, 0))
        y = _silu(x_up * w0 + x * w1 + x_dn * w2)
        q, k, v = y[:, 0:pw], y[:, pw:2 * pw], y[:, 2 * pw:]
        q = q * lax.rsqrt(_head_sum(q * q, ones_b) + EPS) * (HEAD_DIM ** -0.5)
        k = k * lax.rsqrt(_head_sum(k * k, ones_b) + EPS)

        ga = zb_ref[pl.ds(r0, c64), GATE_OFF:GATE_OFF + LANES]
        beta = jax.nn.sigmoid(ga)
        a = pltpu.roll(ga, LANES - N_DIR * B_HEADS, 1) + dtb_ref[...]
        g = neg_decay_rate * (jnp.maximum(a, 0.0) + jnp.log(1.0 + jnp.exp(-jnp.abs(a))))
        cf, cr = g, g
        sh = 1
        while sh < c64:
            cf = cf + jnp.where(row >= sh, pltpu.roll(cf, sh, 0), 0.0)
            cr = cr + jnp.where(row < c64 - sh, pltpu.roll(cr, c64 - sh, 0), 0.0)
            sh *= 2
        gc = jnp.where(lane < B_HEADS, cf, cr)
        g3 = _split3(gc)
        gct = _nt(eye8, g3[0]) + (_nt(eye8, g3[1]) + _nt(eye8, g3[2]))
        b2 = _split2(beta)
        ex = _dot(jnp.concatenate([b2[0], b2[1], g3[0], g3[1], g3[2]], axis=0), expand)
        bexp = ex[0:c64] + ex[c64:2 * c64]
        gexp = ex[2 * c64:3 * c64] + (ex[3 * c64:4 * c64] + ex[4 * c64:])

        k_bd = _bd(k.astype(BF16), bd_mask)
        q_bf = q.astype(BF16)
        probs = []
        for d in range(N_DIR):
            ls = slice(d * pw, (d + 1) * pw)
            bx, gx = bexp[:, ls], gexp[:, ls]
            incl = (ri >= cj) if d == 0 else (ri <= cj)
            strict = (ri > cj) if d == 0 else (ri < cj)
            last = c64 - 1 if d == 0 else 0
            g_row = jnp.concatenate([gct[d * B_HEADS + h:d * B_HEADS + h + 1, :] for h in range(B_HEADS)], axis=1)
            g_last = gx[last:last + 1, :]
            decay = jnp.exp(jnp.where(incl, gx - g_row, NEG_INF))
            kb = k * bx
            aa = _nt(jnp.concatenate([kb.astype(BF16), q_bf], axis=0), k_bd)
            eg = jnp.exp(gx)
            probs.append(dict(
                c=c, d=d, lmat=jnp.where(strict, aa[0:c64] * decay, 0.0),
                a_intra=jnp.where(incl, aa[c64:] * decay, 0.0).astype(BF16),
                vb=v * bx, kbe=kb * eg, qe=q * eg, k_dec=(k * jnp.exp(g_last - gx)).astype(BF16),
                gam=jnp.exp(g_last)))
        return probs

    def terms_loop(i, carry):
        probs = [p for j in range(DELTA_UNROLL) for p in chunk_terms(i * DELTA_UNROLL + j)]
        tinvs = _pk_unit_tri_inverses([p["lmat"] for p in probs], [p["d"] == 0 for p in probs], ri, cj, bd_mask)
        us = [_pk_mm(t, p["vb"], bd_mask).astype(BF16) for t, p in zip(tinvs, probs)]
        ws = [_pk_mm(t, p["kbe"], bd_mask).astype(BF16) for t, p in zip(tinvs, probs)]
        for p, u, w in zip(probs, us, ws):
            c, d = p["c"], p["d"]
            qt_s[c, d] = (p["qe"] - _dot(p["a_intra"], _bd(w, bd_mask))).astype(BF16)
            au_s[c, d] = _dot(p["a_intra"], _bd(u, bd_mask))
            full = _tn(p["k_dec"], jnp.concatenate([w, u], axis=1))
            p_s[c, d] = _unpack_bd(full[:, 0:pw], lane_head).astype(BF16)
            n_s[c, d] = _unpack_bd(full[:, pw:], lane_head)
            gam_s[c, d] = jnp.broadcast_to(p["gam"], (8, pw))
        return carry

    lax.fori_loop(0, n // DELTA_UNROLL, terms_loop, 0)

    for d in range(N_DIR):
        if has_s0:
            s_pk = jnp.concatenate([s0_ref[d, h] for h in range(B_HEADS)], axis=1)
            st_s[d] = _bd(s_pk, bd_mask)
        else:
            st_s[d] = jnp.zeros((pw, pw), F32)

    def scan(t, carry):
        for d in range(N_DIR):
            c = t if d == 0 else n - 1 - t
            r0 = pl.multiple_of(c * c64, c64)
            s = st_s[d]
            s_bf = s.astype(BF16)
            out_s = of_s if d == 0 else ob_s
            out_s[pl.ds(r0, c64), :] = _dot(qt_s[c, d], s_bf) + au_s[c, d]
            st_s[d] = (s * gam_s[c, d, 0:1, :] - _dot(_bd(p_s[c, d], bd_mask), s_bf)) + _bd(n_s[c, d], bd_mask)
        return carry

    lax.fori_loop(0, n, scan, 0)

    fb = DELTA_UNROLL * c64

    def finish(i, carry):
        r0 = pl.multiple_of(i * fb, fb)
        o = of_s[pl.ds(r0, fb), :] + ob_s[pl.ds(r0, fb), :]
        y = _head_rmsnorm(o, gon_ref[...], ones_b)
        o_ref[pl.ds(r0, fb), :] = y * _silu(zb_ref[pl.ds(r0, fb), 3 * B_WIDTH:4 * B_WIDTH])
        return carry

    lax.fori_loop(0, n // DELTA_UNROLL, finish, 0)
    for d in range(N_DIR):
        s_pk = _unpack_bd(st_s[d], lax.broadcasted_iota(jnp.int32, (HEAD_DIM, pw), 1) // HEAD_DIM)
        for h in range(B_HEADS):
            sfin_ref[d, h] = s_pk[:, h * HEAD_DIM:(h + 1) * HEAD_DIM]


def _deltanet(zb, s0, conv_w, a_log, dt_bias, g_onorm, *, nb, seq, layer=None):
    n = seq // CHUNK
    has_s0 = s0 is not None
    pad = LANES - N_DIR * B_HEADS
    alog_row = jnp.pad(a_log.reshape(1, -1).astype(F32), ((0, 0), (0, pad)))
    dtb_row = jnp.pad(dt_bias.reshape(1, -1).astype(F32), ((0, 0), (0, pad)))
    gon_row = jnp.tile(g_onorm.astype(F32), B_HEADS).reshape(1, B_WIDTH)
    st_shape = (N_DIR, B_HEADS, HEAD_DIM, HEAD_DIM)
    args = [zb]
    in_specs = [pl.BlockSpec((seq, ZB_W), lambda b: (b, 0))]
    if has_s0:
        args.append(s0)
        in_specs.append(pl.BlockSpec((None, None) + st_shape, lambda b: (b, layer, 0, 0, 0, 0)))
    args += [conv_w.astype(F32), alog_row, dtb_row, gon_row]
    in_specs += [pl.BlockSpec((3, 3 * B_WIDTH), lambda b: (0, 0)),
                 pl.BlockSpec((1, LANES), lambda b: (0, 0)),
                 pl.BlockSpec((1, LANES), lambda b: (0, 0)),
                 pl.BlockSpec((1, B_WIDTH), lambda b: (0, 0))]
    per_chunk = (n, N_DIR, CHUNK, B_WIDTH)
    return pl.pallas_call(
        functools.partial(_delta_kernel, seq=seq, has_s0=has_s0),
        out_shape=(jax.ShapeDtypeStruct((nb * seq, B_WIDTH), F32),
                   jax.ShapeDtypeStruct((nb,) + st_shape, F32)),
        grid=(nb,),
        in_specs=in_specs,
        out_specs=(pl.BlockSpec((seq, B_WIDTH), lambda b: (b, 0)),
                   pl.BlockSpec((None,) + st_shape, lambda b: (b, 0, 0, 0, 0))),
        scratch_shapes=[pltpu.VMEM(per_chunk, BF16), pltpu.VMEM(per_chunk, F32),
                        pltpu.VMEM(per_chunk, BF16), pltpu.VMEM(per_chunk, F32),
                        pltpu.VMEM((n, N_DIR, 8, B_WIDTH), F32),
                        pltpu.VMEM((seq, B_WIDTH), F32), pltpu.VMEM((seq, B_WIDTH), F32),
                        pltpu.VMEM((N_DIR, B_WIDTH, B_WIDTH), F32)],
        compiler_params=_cparams(("arbitrary",)),
        name="deltanet",
    )(*args)


def _post_kernel(x_ref, oa_ref, ob_ref, oc_ref, mod_ref, gn_ref, ga_ref, gc_ref,
                 wo_ref, wgu_ref, wd_ref, y_ref):
    x = x_ref[...]
    ya = _rmsnorm(oa_ref[...], ga_ref[...])
    yc = _rmsnorm(oc_ref[...], gc_ref[...])
    y = jnp.concatenate([ya, ob_ref[...], yc], axis=-1).astype(BF16)
    m = _dot(y, wo_ref[...])
    x1 = x + mod_ref[0, 2:3, :] * _rmsnorm(m, gn_ref[1:2, :])
    h = _rmsnorm(x1, gn_ref[2:3, :]) * (1.0 + mod_ref[0, 4:5, :]) + mod_ref[0, 3:4, :]
    gu = _dot(h.astype(BF16), wgu_ref[...])
    act = (_silu(gu[:, 0:D_FF]) * gu[:, D_FF:]).astype(BF16)
    f = _dot(act, wd_ref[...])
    y_ref[...] = x1 + mod_ref[0, 5:6, :] * _rmsnorm(f, gn_ref[3:4, :])


def _post(x, oa, ob, oc, mod, g_norm, g_out_a, g_out_c, w_out, w_gu, w_down):
    m, d = x.shape
    tm = TOKEN_TILE
    nb = mod.shape[0]
    per = (m // nb) // tm
    row = lambda i: (i, 0)
    const = lambda i: (0, 0)
    single = pl.Buffered(1)
    return pl.pallas_call(
        _post_kernel,
        out_shape=jax.ShapeDtypeStruct((m, d), F32),
        grid=(m // tm,),
        in_specs=[pl.BlockSpec((tm, d), row),
                  pl.BlockSpec((tm, A_WIDTH), row),
                  pl.BlockSpec((tm, B_WIDTH), row),
                  pl.BlockSpec((tm, C_WIDTH), row),
                  pl.BlockSpec((1, N_MOD, d), lambda i: (i // per, 0, 0)),
                  pl.BlockSpec((4, d), const),
                  pl.BlockSpec((1, A_WIDTH), const),
                  pl.BlockSpec((1, C_WIDTH), const),
                  pl.BlockSpec(w_out.shape, const, pipeline_mode=single),
                  pl.BlockSpec(w_gu.shape, const, pipeline_mode=single),
                  pl.BlockSpec(w_down.shape, const, pipeline_mode=single)],
        out_specs=pl.BlockSpec((tm, d), row),
        compiler_params=_cparams(("arbitrary",)),
        name="post",
    )(x, oa, ob, oc, mod, g_norm, g_out_a, g_out_c, w_out, w_gu, w_down)


def _rope_tables(seq):
    quarter = HEAD_DIM // 4
    inv = ROPE_THETA ** (-jnp.arange(quarter, dtype=F32) / quarter)
    t = jnp.arange(seq)
    rowp = (t // GRID_W).astype(F32)
    colp = (t % GRID_W).astype(F32)
    ang = jnp.concatenate([rowp[:, None] * inv, colp[:, None] * inv], axis=-1)
    cos, sin = jnp.cos(ang), jnp.sin(ang)
    reps = LANES // HEAD_DIM
    return (jnp.tile(jnp.concatenate([cos, cos], axis=-1), (1, reps)),
            jnp.tile(jnp.concatenate([-sin, sin], axis=-1), (1, reps)))


def _split_w_in(w_in):
    n_gate = 2 * N_DIR * B_HEADS
    cut = ZA_W + 4 * B_WIDTH
    w_g = jnp.pad(w_in[..., cut:cut + n_gate], ((0, 0), (0, 0), (0, LANES - n_gate)))
    return w_in[..., :cut].astype(BF16), w_g.astype(BF16), w_in[..., cut + n_gate:].astype(BF16)


def kernel(x_prompt, x_sample, cache_a_k, cache_a_v, state_b, cache_c_k, cache_c_v, c, c_ctx, w_mod, b_mod,
           g_norm, w_in, g_qk_a, g_out_a, conv_w, a_log, dt_bias, g_onorm_b, rpb, g_out_c, w_out, w_gu, w_down):
    nbp, seq_p, d = x_prompt.shape
    nbs, seq_s, _ = x_sample.shape
    depth = w_mod.shape[0]
    past = cache_a_k.shape[2]

    cvec = jnp.concatenate([c_ctx[None, :], c, jnp.zeros((8 - 1 - nbs, d), F32)], axis=0)
    mods = _modulation(cvec, w_mod, b_mod)
    w_ab, w_g, w_c = _split_w_in(w_in)
    w_out_b, w_gu_b, w_down_b = w_out.astype(BF16), w_gu.astype(BF16), w_down.astype(BF16)
    cos, sin = _rope_tables(seq_s)
    tt = _nbr_bias_tables(rpb, seq_s // GRID_W)
    ck_a = cache_a_k.reshape(nbs, depth, past, A_KV_WIDTH)
    cv_a = cache_a_v.reshape(nbs, depth, past, A_KV_WIDTH)
    ck_c = cache_c_k.reshape(nbs, depth, past, C_WIDTH)
    cv_c = cache_c_v.reshape(nbs, depth, past, C_WIDTH)

    xp = x_prompt.reshape(nbp * seq_p, d)
    xs = x_sample.reshape(nbs * seq_s, d)
    ka_l, va_l, sb_l, kc_l, vc_l = [], [], [], [], []
    for l in range(depth):
        gq = jnp.tile(g_qk_a[l, 0], A_HEADS).reshape(1, A_WIDTH)
        gk = jnp.tile(g_qk_a[l, 1], A_KV_HEADS).reshape(1, A_KV_WIDTH)
        goa, goc = g_out_a[l].reshape(1, A_WIDTH), g_out_c[l].reshape(1, C_WIDTH)
        g0 = g_norm[l, 0:1]
        mod_p = mods[l, 0:1].reshape(1, N_MOD, d)
        mod_s = mods[l, 1:1 + nbs].reshape(nbs, N_MOD, d)

        aq, ak, av, zb, cq, ck, cv = _inproj(xp, mod_p, g0, w_ab[l], w_g[l], w_c[l])
        o_a, ka = _attention(aq, ak, av, nb=nbp, seq=seq_p, n_q=A_HEADS, n_kv=A_KV_HEADS, tq=seq_p,
                             gq=gq, gk=gk, emit_k=True)
        o_b, s_b = _deltanet(zb, None, conv_w[l], a_log[l], dt_bias[l], g_onorm_b[l], nb=nbp, seq=seq_p)
        (o_c,) = _attention(cq, ck, cv, nb=nbp, seq=seq_p, n_q=C_HEADS, n_kv=C_HEADS, tq=seq_p)
        xp = _post(xp, o_a, o_b, o_c, mod_p, g_norm[l], goa, goc, w_out_b[l], w_gu_b[l], w_down_b[l])
        ka_l.append(ka.reshape(nbp, seq_p, A_KV_HEADS, HEAD_DIM))
        va_l.append(av.reshape(nbp, seq_p, A_KV_HEADS, HEAD_DIM))
        sb_l.append(s_b)
        kc_l.append(ck.reshape(nbp, seq_p, C_HEADS, HEAD_DIM))
        vc_l.append(cv.reshape(nbp, seq_p, C_HEADS, HEAD_DIM))

        aq, ak, av, zb, cq, ck, cv = _inproj(xs, mod_s, g0, w_ab[l], w_g[l], w_c[l])
        (o_a,) = _attention(aq, ak, av, nb=nbs, seq=seq_s, n_q=A_HEADS, n_kv=A_KV_HEADS, tq=Q_TILE, layer=l,
                            kc=ck_a, vc=cv_a, gq=gq, gk=gk, cos=cos, sin=sin)
        o_b, _ = _deltanet(zb, state_b, conv_w[l], a_log[l], dt_bias[l], g_onorm_b[l], nb=nbs, seq=seq_s, layer=l)
        o_c = _nbr_attention(cq, ck, cv, ck_c, cv_c, tt, nb=nbs, seq=seq_s, layer=l)
        xs = _post(xs, o_a, o_b, o_c, mod_s, g_norm[l], goa, goc, w_out_b[l], w_gu_b[l], w_down_b[l])

    return (xp.reshape(nbp, seq_p, d), xs.reshape(nbs, seq_s, d),
            jnp.stack(ka_l, axis=1), jnp.stack(va_l, axis=1), jnp.stack(sb_l, axis=1),
            jnp.stack(kc_l, axis=1), jnp.stack(vc_l, axis=1))
```

```python
import functools

import numpy as np
import jax
import jax.numpy as jnp
from jax import lax
from jax.experimental import pallas as pl
from jax.experimental.pallas import tpu as pltpu

F32 = jnp.float32
BF16 = jnp.bfloat16

D_MODEL = 1024
HEAD_DIM = 64
GRID_W = 64
A_HEADS, A_KV_HEADS, B_HEADS, C_HEADS = 6, 2, 4, 6
A_WIDTH, A_KV_WIDTH = A_HEADS * HEAD_DIM, A_KV_HEADS * HEAD_DIM
B_WIDTH, C_WIDTH = B_HEADS * HEAD_DIM, C_HEADS * HEAD_DIM
N_DIR = 2
CHUNK = 64
WIN_R, WIN_C = 8, 16
ROPE_THETA = 10000.0
D_FF = 2816
N_MOD = 6
EPS = 1e-6
LANES = 128
ZA_W = A_WIDTH + 2 * A_KV_WIDTH
ZB_W = 3 * B_WIDTH + B_WIDTH + LANES
ZC_W = 3 * C_WIDTH
GATE_OFF = 4 * B_WIDTH
VMEM_LIMIT = 56 * 1024 * 1024
TOKEN_TILE = 512
Q_TILE = 256
DELTA_UNROLL = 4
NBR_G = 4
NBR_KROWS = 12
NEG_INF = float("-inf")


def _cparams(sem):
    return pltpu.CompilerParams(dimension_semantics=sem, vmem_limit_bytes=VMEM_LIMIT)


def _split2(x):
    hi = x.astype(BF16)
    lo = (x - hi.astype(F32)).astype(BF16)
    return hi, lo


def _split3(x):
    hi = x.astype(BF16)
    r1 = x - hi.astype(F32)
    mid = r1.astype(BF16)
    lo = (r1 - mid.astype(F32)).astype(BF16)
    return hi, mid, lo


def _dot(a, b):
    return jnp.dot(a, b, preferred_element_type=F32)


def _nt(a, b):
    return lax.dot_general(a, b, (((1,), (1,)), ((), ())), preferred_element_type=F32)


def _tn(a, b):
    return lax.dot_general(a, b, (((0,), (0,)), ((), ())), preferred_element_type=F32)


def _head_ones(width):
    r = lax.broadcasted_iota(jnp.int32, (width, width), 0) // HEAD_DIM
    c = lax.broadcasted_iota(jnp.int32, (width, width), 1) // HEAD_DIM
    return jnp.where(r == c, 1.0, 0.0).astype(BF16)


def _head_sum(x, ones):
    hi, lo = _split2(x)
    return _dot(hi, ones) + _dot(lo, ones)


def _head_rmsnorm(x, g_row, ones):
    ms = _head_sum(x * x, ones) * (1.0 / HEAD_DIM)
    return x * lax.rsqrt(ms + EPS) * g_row


def _rmsnorm(x, g_row):
    ms = jnp.mean(x * x, axis=-1, keepdims=True)
    return x * lax.rsqrt(ms + EPS) * g_row


def _silu(x):
    return x * jax.nn.sigmoid(x)


def _rope(x, cos, sin_signed, width):
    reps = width // LANES
    if reps > 1:
        cos = jnp.concatenate([cos] * reps, axis=1)
        sin_signed = jnp.concatenate([sin_signed] * reps, axis=1)
    lane = lax.broadcasted_iota(jnp.int32, x.shape, 1)
    first = (lane & (HEAD_DIM // 2)) == 0
    partner = jnp.where(first, pltpu.roll(x, width - HEAD_DIM // 2, 1), pltpu.roll(x, HEAD_DIM // 2, 1))
    return x * cos + partner * sin_signed


def _mod_kernel(c_ref, w_ref, b_ref, o_ref):
    a = _silu(c_ref[...]).astype(BF16)
    o_ref[0] = _dot(a, w_ref[0].astype(BF16)) + b_ref[0]


def _modulation(cvec8, w_mod, b_mod):
    depth, d, n = w_mod.shape
    tn = 1536
    return pl.pallas_call(
        _mod_kernel,
        out_shape=jax.ShapeDtypeStruct((depth, 8, n), F32),
        grid=(depth, n // tn),
        in_specs=[pl.BlockSpec((8, d), lambda l, j: (0, 0)),
                  pl.BlockSpec((1, d, tn), lambda l, j: (l, 0, j)),
                  pl.BlockSpec((1, 1, tn), lambda l, j: (l, 0, j))],
        out_specs=pl.BlockSpec((1, 8, tn), lambda l, j: (l, 0, j)),
        compiler_params=_cparams(("arbitrary", "arbitrary")),
        name="modulation",
    )(cvec8, w_mod, b_mod.reshape(depth, 1, n))


INPROJ_OUT_W = (A_WIDTH, A_KV_WIDTH, A_KV_WIDTH, ZB_W, C_WIDTH, C_WIDTH, C_WIDTH)


def _inproj_kernel(x_ref, mod_ref, g_ref, wab_ref, wg_ref, wc_ref,
                   aq_ref, ak_ref, av_ref, zb_ref, cq_ref, ck_ref, cv_ref):
    x = x_ref[...]
    h = (_rmsnorm(x, g_ref[...]) * (1.0 + mod_ref[0, 1:2, :]) + mod_ref[0, 0:1, :]).astype(BF16)
    z = _dot(h, wab_ref[...])
    aq_ref[...] = z[:, 0:A_WIDTH]
    ak_ref[...] = z[:, A_WIDTH:A_WIDTH + A_KV_WIDTH]
    av_ref[...] = z[:, A_WIDTH + A_KV_WIDTH:ZA_W]
    zb_ref[:, 0:GATE_OFF] = z[:, ZA_W:]
    zb_ref[:, GATE_OFF:] = _dot(h, wg_ref[...])
    zc = _dot(h, wc_ref[...])
    cq_ref[...] = zc[:, 0:C_WIDTH]
    ck_ref[...] = zc[:, C_WIDTH:2 * C_WIDTH]
    cv_ref[...] = zc[:, 2 * C_WIDTH:]


def _inproj(x, mod, g_row, w_ab, w_g, w_c):
    m, d = x.shape
    tm = TOKEN_TILE
    nb = mod.shape[0]
    per = (m // nb) // tm
    const = lambda i: (0, 0)
    single = pl.Buffered(1)
    return pl.pallas_call(
        _inproj_kernel,
        out_shape=tuple(jax.ShapeDtypeStruct((m, w), F32) for w in INPROJ_OUT_W),
        grid=(m // tm,),
        in_specs=[pl.BlockSpec((tm, d), lambda i: (i, 0)),
                  pl.BlockSpec((1, N_MOD, d), lambda i: (i // per, 0, 0)),
                  pl.BlockSpec((1, d), const),
                  pl.BlockSpec(w_ab.shape, const, pipeline_mode=single),
                  pl.BlockSpec(w_g.shape, const, pipeline_mode=single),
                  pl.BlockSpec(w_c.shape, const, pipeline_mode=single)],
        out_specs=tuple(pl.BlockSpec((tm, w), lambda i: (i, 0)) for w in INPROJ_OUT_W),
        compiler_params=_cparams(("arbitrary",)),
        name="inproj",
    )(x, mod, g_row, w_ab, w_g, w_c)


def _attn_kernel(*refs, n_q, n_kv, tq, l_self, l_ctx, norm, rope, emit_k):
    it = iter(refs)
    q_ref, k_ref, v_ref = next(it), next(it), next(it)
    kc_ref = vc_ref = gq_ref = gk_ref = cos_ref = sin_ref = kout_ref = None
    if l_ctx:
        kc_ref, vc_ref = next(it), next(it)
    if norm:
        gq_ref, gk_ref = next(it), next(it)
    if rope:
        cos_ref, sin_ref = next(it), next(it)
    o_ref = next(it)
    if emit_k:
        kout_ref = next(it)
    kbuf, vbuf = next(it), next(it)

    j = pl.program_id(1)
    wq, wk = n_q * HEAD_DIM, n_kv * HEAD_DIM
    grp = n_q // n_kv
    rb = 256

    @pl.when(j == 0)
    def _():
        ones_k = _head_ones(wk) if norm else None

        def body(i, carry):
            r0 = pl.multiple_of(i * rb, rb)
            k = k_ref[pl.ds(r0, rb), :]
            if norm:
                k = _head_rmsnorm(k, gk_ref[...], ones_k)
            if rope:
                k = _rope(k, cos_ref[pl.ds(r0, rb), :], sin_ref[pl.ds(r0, rb), :], wk)
            if emit_k:
                kout_ref[pl.ds(r0, rb), :] = k
            kbuf[pl.ds(r0, rb), :] = k.astype(BF16)
            vbuf[pl.ds(r0, rb), :] = v_ref[pl.ds(r0, rb), :].astype(BF16)
            return carry
        lax.fori_loop(0, l_self // rb, body, 0)
        if l_ctx:
            kbuf[l_self:l_self + l_ctx, :] = kc_ref[...].astype(BF16)
            vbuf[l_self:l_self + l_ctx, :] = vc_ref[...].astype(BF16)

    q = q_ref[...]
    if norm:
        q = _head_rmsnorm(q, gq_ref[...], _head_ones(wq))
    if rope:
        r0 = pl.multiple_of(j * tq, tq)
        q = _rope(q, cos_ref[pl.ds(r0, tq), :], sin_ref[pl.ds(r0, tq), :], wq)
    q = (q * (HEAD_DIM ** -0.5)).astype(BF16)
    for h in range(n_q):
        kv = h // grp
        qh = q[:, h * HEAD_DIM:(h + 1) * HEAD_DIM]
        s = _nt(qh, kbuf[:, kv * HEAD_DIM:(kv + 1) * HEAD_DIM])
        m = jnp.max(s, axis=-1, keepdims=True)
        p = jnp.exp(s - m)
        l = jnp.sum(p, axis=-1, keepdims=True)
        o = _dot(p.astype(BF16), vbuf[:, kv * HEAD_DIM:(kv + 1) * HEAD_DIM])
        o_ref[:, h * HEAD_DIM:(h + 1) * HEAD_DIM] = o / l


def _attention(q, k, v, *, nb, seq, n_q, n_kv, tq, layer=None, kc=None, vc=None, gq=None, gk=None,
               cos=None, sin=None, emit_k=False):
    wq, wk = n_q * HEAD_DIM, n_kv * HEAD_DIM
    nq = seq // tq
    l_ctx = 0 if kc is None else kc.shape[2]
    norm, rope = gq is not None, cos is not None
    args = [q, k, v]
    in_specs = [pl.BlockSpec((tq, wq), lambda b, j: (b * nq + j, 0)),
                pl.BlockSpec((seq, wk), lambda b, j: (b, 0)),
                pl.BlockSpec((seq, wk), lambda b, j: (b, 0))]
    if l_ctx:
        args += [kc, vc]
        in_specs += [pl.BlockSpec((None, None, l_ctx, wk), lambda b, j: (b, layer, 0, 0))] * 2
    if norm:
        args += [gq, gk]
        in_specs += [pl.BlockSpec((1, wq), lambda b, j: (0, 0)), pl.BlockSpec((1, wk), lambda b, j: (0, 0))]
    if rope:
        args += [cos, sin]
        in_specs += [pl.BlockSpec((seq, LANES), lambda b, j: (0, 0))] * 2
    out_shape = [jax.ShapeDtypeStruct((nb * seq, wq), F32)]
    out_specs = [pl.BlockSpec((tq, wq), lambda b, j: (b * nq + j, 0))]
    if emit_k:
        out_shape.append(jax.ShapeDtypeStruct((nb * seq, wk), F32))
        out_specs.append(pl.BlockSpec((seq, wk), lambda b, j: (b, 0)))
    kern = functools.partial(_attn_kernel, n_q=n_q, n_kv=n_kv, tq=tq, l_self=seq, l_ctx=l_ctx,
                             norm=norm, rope=rope, emit_k=emit_k)
    return pl.pallas_call(
        kern,
        out_shape=tuple(out_shape),
        grid=(nb, nq),
        in_specs=in_specs,
        out_specs=tuple(out_specs),
        scratch_shapes=[pltpu.VMEM((seq + l_ctx, wk), BF16), pltpu.VMEM((seq + l_ctx, wk), BF16)],
        compiler_params=_cparams(("arbitrary", "arbitrary")),
        name="attention",
    )(*args)


def _nbr_plan(rows):
    nblk = rows // NBR_G
    kstart, pairs, ent = [], {}, []
    for blk in range(nblk):
        r0 = blk * NBR_G
        ks = min(max(r0 - WIN_R // 2, 0), rows - NBR_KROWS)
        kstart.append(ks)
        for i in range(NBR_G):
            r = r0 + i
            rs = min(max(r - WIN_R // 2, 0), rows - WIN_R)
            assert ks <= rs and rs + WIN_R <= ks + NBR_KROWS
            for p in range(NBR_KROWS // 2):
                ab = []
                for krow in (ks + 2 * p, ks + 2 * p + 1):
                    valid = rs <= krow < rs + WIN_R
                    ab.append(krow - r + WIN_R if valid else 0)
                ent.append(pairs.setdefault(tuple(ab), len(pairs)))
    pair_list = sorted(pairs, key=pairs.get)
    return (np.asarray(kstart, np.int32), np.asarray(ent, np.int32),
            np.asarray([a for a, _ in pair_list], np.int32), np.asarray([b for _, b in pair_list], np.int32))


def _nbr_bias_tables(rpb, rows):
    _, _, left, right = _nbr_plan(rows)
    col = np.arange(GRID_W)
    c_start = np.clip(col - WIN_C // 2, 0, GRID_W - WIN_C)
    cmask = (col[None, :] >= c_start[:, None]) & (col[None, :] < c_start[:, None] + WIN_C)
    dc = np.clip(col[None, :] - col[:, None] + (WIN_C - 1), 0, 2 * WIN_C - 2)
    onehot = (dc.reshape(-1)[None, :] == np.arange(2 * WIN_C - 1)[:, None]).astype(np.float32)
    t = jnp.einsum("lhde,ek->lhdk", rpb.astype(F32), jnp.asarray(onehot), precision=lax.Precision.HIGHEST)
    t = t.reshape(rpb.shape[:3] + (GRID_W, GRID_W))
    t = jnp.where(jnp.asarray(cmask), t, NEG_INF)
    masked = jnp.full(t.shape[:2] + (1, GRID_W, GRID_W), NEG_INF, F32)
    tpad = jnp.concatenate([masked, t], axis=2)
    return jnp.concatenate([jnp.take(tpad, jnp.asarray(left), axis=2),
                            jnp.take(tpad, jnp.asarray(right), axis=2)], axis=-1)


def _nbr_kernel(ks_ref, ent_ref, q_ref, k_ref, v_ref, kc_ref, vc_ref, tt_ref, o_ref,
                kbuf, vbuf, kcbuf, vcbuf, s_ref, *, seq):
    blk = pl.program_id(1)
    nk = NBR_KROWS * GRID_W
    npair = NBR_KROWS // 2
    rb = 256

    @pl.when(blk == 0)
    def _():
        def body(i, carry):
            r0 = pl.multiple_of(i * rb, rb)
            kbuf[pl.ds(r0, rb), :] = k_ref[pl.ds(r0, rb), :].astype(BF16)
            vbuf[pl.ds(r0, rb), :] = v_ref[pl.ds(r0, rb), :].astype(BF16)
            return carry
        lax.fori_loop(0, seq // rb, body, 0)
        kcbuf[...] = kc_ref[...].astype(BF16)
        vcbuf[...] = vc_ref[...].astype(BF16)

    ks = pl.multiple_of(ks_ref[blk] * GRID_W, GRID_W)
    q = (q_ref[...] * (HEAD_DIM ** -0.5)).astype(BF16)
    for h in range(C_HEADS):
        hs = slice(h * HEAD_DIM, (h + 1) * HEAD_DIM)
        qh = q[:, hs]
        s_ref[...] = _nt(qh, kbuf[pl.ds(ks, nk), hs])
        for i in range(NBR_G):
            for p in range(npair):
                e = ent_ref[(blk * NBR_G + i) * npair + p]
                rs_, cs_ = slice(i * GRID_W, (i + 1) * GRID_W), slice(p * LANES, (p + 1) * LANES)
                s_ref[rs_, cs_] = s_ref[rs_, cs_] + tt_ref[h, e]
        s_loc = s_ref[...]
        s_ctx = _nt(qh, kcbuf[:, hs])
        m = jnp.maximum(jnp.max(s_loc, axis=-1, keepdims=True), jnp.max(s_ctx, axis=-1, keepdims=True))
        p_loc = jnp.exp(s_loc - m)
        p_ctx = jnp.exp(s_ctx - m)
        l = jnp.sum(p_loc, axis=-1, keepdims=True) + jnp.sum(p_ctx, axis=-1, keepdims=True)
        o = _dot(p_loc.astype(BF16), vbuf[pl.ds(ks, nk), hs]) + _dot(p_ctx.astype(BF16), vcbuf[:, hs])
        o_ref[:, hs] = o / l


def _nbr_attention(q, k, v, kc, vc, tt, *, nb, seq, layer):
    rows = seq // GRID_W
    kstart, ent, _, _ = _nbr_plan(rows)
    nblk = rows // NBR_G
    tq = NBR_G * GRID_W
    nk = NBR_KROWS * GRID_W
    l_ctx = kc.shape[2]
    n_ent = tt.shape[2]
    smem = pl.BlockSpec(memory_space=pltpu.SMEM)
    return pl.pallas_call(
        functools.partial(_nbr_kernel, seq=seq),
        out_shape=jax.ShapeDtypeStruct((nb * seq, C_WIDTH), F32),
        grid=(nb, nblk),
        in_specs=[smem, smem,
                  pl.BlockSpec((tq, C_WIDTH), lambda b, j: (b * nblk + j, 0)),
                  pl.BlockSpec((seq, C_WIDTH), lambda b, j: (b, 0)),
                  pl.BlockSpec((seq, C_WIDTH), lambda b, j: (b, 0)),
                  pl.BlockSpec((None, None, l_ctx, C_WIDTH), lambda b, j: (b, layer, 0, 0)),
                  pl.BlockSpec((None, None, l_ctx, C_WIDTH), lambda b, j: (b, layer, 0, 0)),
                  pl.BlockSpec((None, C_HEADS, n_ent, GRID_W, LANES), lambda b, j: (layer, 0, 0, 0, 0))],
        out_specs=pl.BlockSpec((tq, C_WIDTH), lambda b, j: (b * nblk + j, 0)),
        scratch_shapes=[pltpu.VMEM((seq, C_WIDTH), BF16), pltpu.VMEM((seq, C_WIDTH), BF16),
                        pltpu.VMEM((l_ctx, C_WIDTH), BF16), pltpu.VMEM((l_ctx, C_WIDTH), BF16),
                        pltpu.VMEM((tq, nk), F32)],
        compiler_params=_cparams(("arbitrary", "arbitrary")),
        name="nbr_attention",
    )(jnp.asarray(kstart), jnp.asarray(ent), q, k, v, kc, vc, tt)


def _bd(x, mask):
    return jnp.concatenate([x] * B_HEADS, axis=0) * mask[jnp.dtype(x.dtype).name]


def _unpack_bd(full, lane_masks):
    out = None
    for g in range(B_HEADS):
        blk = full[g * HEAD_DIM:(g + 1) * HEAD_DIM, :] * lane_masks[g]
        out = blk if out is None else out + blk
    return out


def _pk_mm(a, b, mask):
    return _dot(a.astype(BF16), _bd(b.astype(BF16), mask))


def _pk_unit_tri_inverses(lmats, lowers, ri, cj, mask):
    eye = jnp.where(ri == cj, 1.0, 0.0).astype(F32)
    ts = [eye] * len(lmats)
    s = 1
    while s < CHUNK:
        same_pair = ((ri ^ cj) & ~(2 * s - 1)) == 0
        r_hi, c_hi = (ri & s) != 0, (cj & s) != 0
        sel = {True: jnp.where(same_pair & r_hi & ~c_hi, 1.0, 0.0), False: jnp.where(same_pair & ~r_hi & c_hi, 1.0, 0.0)}
        offs = [lm * sel[lo] for lm, lo in zip(lmats, lowers)]
        if s == 1:
            ts = [t - off for t, off in zip(ts, offs)]
        else:
            mids = [_pk_mm(t, off, mask) for t, off in zip(ts, offs)]
            ts = [t - _pk_mm(m, t, mask) for t, m in zip(ts, mids)]
        s *= 2
    return ts


def _delta_kernel(*refs, seq, has_s0):
    it = iter(refs)
    zb_ref = next(it)
    s0_ref = next(it) if has_s0 else None
    convw_ref, alog_ref, dtb_ref, gon_ref = next(it), next(it), next(it), next(it)
    o_ref, sfin_ref = next(it), next(it)
    qt_s, au_s, p_s, n_s, gam_s, of_s, ob_s, st_s = (next(it) for _ in range(8))

    n = seq // CHUNK
    c64 = CHUNK
    pw = B_WIDTH
    row = lax.broadcasted_iota(jnp.int32, (c64, 1), 0)
    lane = lax.broadcasted_iota(jnp.int32, (c64, LANES), 1)
    ri = lax.broadcasted_iota(jnp.int32, (c64, pw), 0)
    lane_p = lax.broadcasted_iota(jnp.int32, (c64, pw), 1)
    cj = lane_p & (HEAD_DIM - 1)
    lane_head = [jnp.where(lane_p // HEAD_DIM == g, 1.0, 0.0) for g in range(B_HEADS)]
    off_diag = jnp.where(ri == cj, 0.0, 1.0)
    bd_bool = (lax.broadcasted_iota(jnp.int32, (pw, pw), 0) // HEAD_DIM
               == lax.broadcasted_iota(jnp.int32, (pw, pw), 1) // HEAD_DIM)
    bd_mask = {"float32": jnp.where(bd_bool, 1.0, 0.0), "bfloat16": jnp.where(bd_bool, 1.0, 0.0).astype(BF16)}
    ones_b = _head_ones(pw)
    w0, w1, w2 = convw_ref[0:1, :], convw_ref[1:2, :], convw_ref[2:3, :]
    neg_decay_rate = -jnp.exp(alog_ref[...])
    eye8 = jnp.where(lax.broadcasted_iota(jnp.int32, (8, LANES), 0) == lax.broadcasted_iota(jnp.int32, (8, LANES), 1),
                     1.0, 0.0).astype(BF16)
    expand = jnp.where(lax.broadcasted_iota(jnp.int32, (LANES, N_DIR * pw), 1) // HEAD_DIM
                       == lax.broadcasted_iota(jnp.int32, (LANES, N_DIR * pw), 0), 1.0, 0.0).astype(BF16)

    def chunk_terms(c):
        r0 = pl.multiple_of(c * c64, c64)
        x = zb_ref[pl.ds(r0, c64), 0:3 * B_WIDTH]
        prev = zb_ref[pl.ds(jnp.maximum(r0 - 1, 0), 1), 0:3 * B_WIDTH] * jnp.where(c > 0, 1.0, 0.0)
        nxt = zb_ref[pl.ds(jnp.minimum(r0 + c64, seq - 1), 1), 0:3 * B_WIDTH] * jnp.where(c < n - 1, 1.0, 0.0)
        x_up = jnp.where(row == 0, prev, pltpu.roll(x, 1, 0))
        x_dn = jnp.where(row == c64 - 1, nxt, pltpu.roll(x, c64 - 1, 0))
        y = _silu(x_up * w0 + x * w1 + x_dn * w2)
        q, k, v = y[:, 0:pw], y[:, pw:2 * pw], y[:, 2 * pw:]
        q = q * lax.rsqrt(_head_sum(q * q, ones_b) + EPS) * (HEAD_DIM ** -0.5)
        k = k * lax.rsqrt(_head_sum(k * k, ones_b) + EPS)

        ga = zb_ref[pl.ds(r0, c64), GATE_OFF:GATE_OFF + LANES]
        beta = jax.nn.sigmoid(ga)
        a = pltpu.roll(ga, LANES - N_DIR * B_HEADS, 1) + dtb_ref[...]
        g = neg_decay_rate * (jnp.maximum(a, 0.0) + jnp.log(1.0 + jnp.exp(-jnp.abs(a))))
        cf, cr = g, g
        sh = 1
        while sh < c64:
            cf = cf + jnp.where(row >= sh, pltpu.roll(cf, sh, 0), 0.0)
            cr = cr + jnp.where(row < c64 - sh, pltpu.roll(cr, c64 - sh, 0), 0.0)
            sh *= 2
        gc = jnp.where(lane < B_HEADS, cf, cr)
        g3 = _split3(gc)
        gct = _nt(eye8, g3[0]) + (_nt(eye8, g3[1]) + _nt(eye8, g3[2]))
        b2 = _split2(beta)
        ex = _dot(jnp.concatenate([b2[0], b2[1], g3[0], g3[1], g3[2]], axis=0), expand)
        bexp = ex[0:c64] + ex[c64:2 * c64]
        gexp = ex[2 * c64:3 * c64] + (ex[3 * c64:4 * c64] + ex[4 * c64:])

        k_bd = _bd(k.astype(BF16), bd_mask)
        q_bf = q.astype(BF16)
        probs = []
        for d in range(N_DIR):
            ls = slice(d * pw, (d + 1) * pw)
            bx, gx = bexp[:, ls], gexp[:, ls]
            incl = (ri >= cj) if d == 0 else (ri <= cj)
            last = c64 - 1 if d == 0 else 0
            g_row = jnp.concatenate([gct[d * B_HEADS + h:d * B_HEADS + h + 1, :] for h in range(B_HEADS)], axis=1)
            g_last = gx[last:last + 1, :]
            decay = jnp.exp(jnp.where(incl, gx - g_row, NEG_INF))
            kb = k * bx
            aa = _nt(jnp.concatenate([kb.astype(BF16), q_bf], axis=0), k_bd)
            eg = jnp.exp(gx)
            probs.append(dict(
                c=c, d=d, lmat=aa[0:c64] * (decay * off_diag),
                a_intra=(aa[c64:] * decay).astype(BF16),
                vb=v * bx, kbe=kb * eg, qe=q * eg, k_dec=(k * jnp.exp(g_last - gx)).astype(BF16),
                gam=jnp.exp(g_last)))
        return probs

    def terms_loop(i, carry):
        probs = [p for j in range(DELTA_UNROLL) for p in chunk_terms(i * DELTA_UNROLL + j)]
        tinvs = _pk_unit_tri_inverses([p["lmat"] for p in probs], [p["d"] == 0 for p in probs], ri, cj, bd_mask)
        us = [_pk_mm(t, p["vb"], bd_mask).astype(BF16) for t, p in zip(tinvs, probs)]
        ws = [_pk_mm(t, p["kbe"], bd_mask).astype(BF16) for t, p in zip(tinvs, probs)]
        for p, u, w in zip(probs, us, ws):
            c, d = p["c"], p["d"]
            qt_s[c, d] = (p["qe"] - _dot(p["a_intra"], _bd(w, bd_mask))).astype(BF16)
            au_s[c, d] = _dot(p["a_intra"], _bd(u, bd_mask))
            full = _tn(p["k_dec"], jnp.concatenate([w, u], axis=1))
            p_s[c, d] = _unpack_bd(full[:, 0:pw], lane_head).astype(BF16)
            n_s[c, d] = _unpack_bd(full[:, pw:], lane_head)
            gam_s[c, d] = jnp.broadcast_to(p["gam"], (8, pw))
        return carry

    lax.fori_loop(0, n // DELTA_UNROLL, terms_loop, 0)

    for d in range(N_DIR):
        if has_s0:
            s_pk = jnp.concatenate([s0_ref[d, h] for h in range(B_HEADS)], axis=1)
            st_s[d] = _bd(s_pk, bd_mask)
        else:
            st_s[d] = jnp.zeros((pw, pw), F32)

    def scan(t, carry):
        for d in range(N_DIR):
            c = t if d == 0 else n - 1 - t
            r0 = pl.multiple_of(c * c64, c64)
            s = st_s[d]
            s_bf = s.astype(BF16)
            out_s = of_s if d == 0 else ob_s
            out_s[pl.ds(r0, c64), :] = _dot(qt_s[c, d], s_bf) + au_s[c, d]
            st_s[d] = (s * gam_s[c, d, 0:1, :] - _dot(_bd(p_s[c, d], bd_mask), s_bf)) + _bd(n_s[c, d], bd_mask)
        return carry

    lax.fori_loop(0, n, scan, 0)

    fb = DELTA_UNROLL * c64

    def finish(i, carry):
        r0 = pl.multiple_of(i * fb, fb)
        o = of_s[pl.ds(r0, fb), :] + ob_s[pl.ds(r0, fb), :]
        y = _head_rmsnorm(o, gon_ref[...], ones_b)
        o_ref[pl.ds(r0, fb), :] = y * _silu(zb_ref[pl.ds(r0, fb), 3 * B_WIDTH:4 * B_WIDTH])
        return carry

    lax.fori_loop(0, n // DELTA_UNROLL, finish, 0)
    for d in range(N_DIR):
        s_pk = _unpack_bd(st_s[d], lane_head)
        for h in range(B_HEADS):
            sfin_ref[d, h] = s_pk[:, h * HEAD_DIM:(h + 1) * HEAD_DIM]


def _deltanet(zb, s0, conv_w, a_log, dt_bias, g_onorm, *, nb, seq, layer=None):
    n = seq // CHUNK
    has_s0 = s0 is not None
    pad = LANES - N_DIR * B_HEADS
    alog_row = jnp.pad(a_log.reshape(1, -1).astype(F32), ((0, 0), (0, pad)))
    dtb_row = jnp.pad(dt_bias.reshape(1, -1).astype(F32), ((0, 0), (0, pad)))
    gon_row = jnp.tile(g_onorm.astype(F32), B_HEADS).reshape(1, B_WIDTH)
    st_shape = (N_DIR, B_HEADS, HEAD_DIM, HEAD_DIM)
    args = [zb]
    in_specs = [pl.BlockSpec((seq, ZB_W), lambda b: (b, 0))]
    if has_s0:
        args.append(s0)
        in_specs.append(pl.BlockSpec((None, None) + st_shape, lambda b: (b, layer, 0, 0, 0, 0)))
    args += [conv_w.astype(F32), alog_row, dtb_row, gon_row]
    in_specs += [pl.BlockSpec((3, 3 * B_WIDTH), lambda b: (0, 0)),
                 pl.BlockSpec((1, LANES), lambda b: (0, 0)),
                 pl.BlockSpec((1, LANES), lambda b: (0, 0)),
                 pl.BlockSpec((1, B_WIDTH), lambda b: (0, 0))]
    per_chunk = (n, N_DIR, CHUNK, B_WIDTH)
    return pl.pallas_call(
        functools.partial(_delta_kernel, seq=seq, has_s0=has_s0),
        out_shape=(jax.ShapeDtypeStruct((nb * seq, B_WIDTH), F32),
                   jax.ShapeDtypeStruct((nb,) + st_shape, F32)),
        grid=(nb,),
        in_specs=in_specs,
        out_specs=(pl.BlockSpec((seq, B_WIDTH), lambda b: (b, 0)),
                   pl.BlockSpec((None,) + st_shape, lambda b: (b, 0, 0, 0, 0))),
        scratch_shapes=[pltpu.VMEM(per_chunk, BF16), pltpu.VMEM(per_chunk, F32),
                        pltpu.VMEM(per_chunk, BF16), pltpu.VMEM(per_chunk, F32),
                        pltpu.VMEM((n, N_DIR, 8, B_WIDTH), F32),
                        pltpu.VMEM((seq, B_WIDTH), F32), pltpu.VMEM((seq, B_WIDTH), F32),
                        pltpu.VMEM((N_DIR, B_WIDTH, B_WIDTH), F32)],
        compiler_params=_cparams(("arbitrary",)),
        name="deltanet",
    )(*args)


def _post_kernel(x_ref, oa_ref, ob_ref, oc_ref, mod_ref, gn_ref, ga_ref, gc_ref,
                 wo_ref, wgu_ref, wd_ref, y_ref):
    x = x_ref[...]
    ya = _rmsnorm(oa_ref[...], ga_ref[...])
    yc = _rmsnorm(oc_ref[...], gc_ref[...])
    y = jnp.concatenate([ya, ob_ref[...], yc], axis=-1).astype(BF16)
    m = _dot(y, wo_ref[...])
    x1 = x + mod_ref[0, 2:3, :] * _rmsnorm(m, gn_ref[1:2, :])
    h = _rmsnorm(x1, gn_ref[2:3, :]) * (1.0 + mod_ref[0, 4:5, :]) + mod_ref[0, 3:4, :]
    gu = _dot(h.astype(BF16), wgu_ref[...])
    act = (_silu(gu[:, 0:D_FF]) * gu[:, D_FF:]).astype(BF16)
    f = _dot(act, wd_ref[...])
    y_ref[...] = x1 + mod_ref[0, 5:6, :] * _rmsnorm(f, gn_ref[3:4, :])


def _post(x, oa, ob, oc, mod, g_norm, g_out_a, g_out_c, w_out, w_gu, w_down):
    m, d = x.shape
    tm = TOKEN_TILE
    nb = mod.shape[0]
    per = (m // nb) // tm
    row = lambda i: (i, 0)
    const = lambda i: (0, 0)
    single = pl.Buffered(1)
    return pl.pallas_call(
        _post_kernel,
        out_shape=jax.ShapeDtypeStruct((m, d), F32),
        grid=(m // tm,),
        in_specs=[pl.BlockSpec((tm, d), row),
                  pl.BlockSpec((tm, A_WIDTH), row),
                  pl.BlockSpec((tm, B_WIDTH), row),
                  pl.BlockSpec((tm, C_WIDTH), row),
                  pl.BlockSpec((1, N_MOD, d), lambda i: (i // per, 0, 0)),
                  pl.BlockSpec((4, d), const),
                  pl.BlockSpec((1, A_WIDTH), const),
                  pl.BlockSpec((1, C_WIDTH), const),
                  pl.BlockSpec(w_out.shape, const, pipeline_mode=single),
                  pl.BlockSpec(w_gu.shape, const, pipeline_mode=single),
                  pl.BlockSpec(w_down.shape, const, pipeline_mode=single)],
        out_specs=pl.BlockSpec((tm, d), row),
        compiler_params=_cparams(("arbitrary",)),
        name="post",
    )(x, oa, ob, oc, mod, g_norm, g_out_a, g_out_c, w_out, w_gu, w_down)


def _rope_tables(seq):
    quarter = HEAD_DIM // 4
    inv = ROPE_THETA ** (-jnp.arange(quarter, dtype=F32) / quarter)
    t = jnp.arange(seq)
    rowp = (t // GRID_W).astype(F32)
    colp = (t % GRID_W).astype(F32)
    ang = jnp.concatenate([rowp[:, None] * inv, colp[:, None] * inv], axis=-1)
    cos, sin = jnp.cos(ang), jnp.sin(ang)
    reps = LANES // HEAD_DIM
    return (jnp.tile(jnp.concatenate([cos, cos], axis=-1), (1, reps)),
            jnp.tile(jnp.concatenate([-sin, sin], axis=-1), (1, reps)))


def _split_w_in(w_in):
    n_gate = 2 * N_DIR * B_HEADS
    cut = ZA_W + 4 * B_WIDTH
    w_g = jnp.pad(w_in[..., cut:cut + n_gate], ((0, 0), (0, 0), (0, LANES - n_gate)))
    return w_in[..., :cut].astype(BF16), w_g.astype(BF16), w_in[..., cut + n_gate:].astype(BF16)


def kernel(x_prompt, x_sample, cache_a_k, cache_a_v, state_b, cache_c_k, cache_c_v, c, c_ctx, w_mod, b_mod,
           g_norm, w_in, g_qk_a, g_out_a, conv_w, a_log, dt_bias, g_onorm_b, rpb, g_out_c, w_out, w_gu, w_down):
    nbp, seq_p, d = x_prompt.shape
    nbs, seq_s, _ = x_sample.shape
    depth = w_mod.shape[0]
    past = cache_a_k.shape[2]

    cvec = jnp.concatenate([c_ctx[None, :], c, jnp.zeros((8 - 1 - nbs, d), F32)], axis=0)
    mods = _modulation(cvec, w_mod, b_mod)
    w_ab, w_g, w_c = _split_w_in(w_in)
    w_out_b, w_gu_b, w_down_b = w_out.astype(BF16), w_gu.astype(BF16), w_down.astype(BF16)
    cos, sin = _rope_tables(seq_s)
    tt = _nbr_bias_tables(rpb, seq_s // GRID_W)
    ck_a = cache_a_k.reshape(nbs, depth, past, A_KV_WIDTH)
    cv_a = cache_a_v.reshape(nbs, depth, past, A_KV_WIDTH)
    ck_c = cache_c_k.reshape(nbs, depth, past, C_WIDTH)
    cv_c = cache_c_v.reshape(nbs, depth, past, C_WIDTH)

    xp = x_prompt.reshape(nbp * seq_p, d)
    xs = x_sample.reshape(nbs * seq_s, d)
    ka_l, va_l, sb_l, kc_l, vc_l = [], [], [], [], []
    for l in range(depth):
        gq = jnp.tile(g_qk_a[l, 0], A_HEADS).reshape(1, A_WIDTH)
        gk = jnp.tile(g_qk_a[l, 1], A_KV_HEADS).reshape(1, A_KV_WIDTH)
        goa, goc = g_out_a[l].reshape(1, A_WIDTH), g_out_c[l].reshape(1, C_WIDTH)
        g0 = g_norm[l, 0:1]
        mod_p = mods[l, 0:1].reshape(1, N_MOD, d)
        mod_s = mods[l, 1:1 + nbs].reshape(nbs, N_MOD, d)

        aq, ak, av, zb, cq, ck, cv = _inproj(xp, mod_p, g0, w_ab[l], w_g[l], w_c[l])
        o_a, ka = _attention(aq, ak, av, nb=nbp, seq=seq_p, n_q=A_HEADS, n_kv=A_KV_HEADS, tq=seq_p,
                             gq=gq, gk=gk, emit_k=True)
        o_b, s_b = _deltanet(zb, None, conv_w[l], a_log[l], dt_bias[l], g_onorm_b[l], nb=nbp, seq=seq_p)
        (o_c,) = _attention(cq, ck, cv, nb=nbp, seq=seq_p, n_q=C_HEADS, n_kv=C_HEADS, tq=seq_p)
        xp = _post(xp, o_a, o_b, o_c, mod_p, g_norm[l], goa, goc, w_out_b[l], w_gu_b[l], w_down_b[l])
        ka_l.append(ka.reshape(nbp, seq_p, A_KV_WIDTH))
        va_l.append(av.reshape(nbp, seq_p, A_KV_WIDTH))
        sb_l.append(s_b)
        kc_l.append(ck.reshape(nbp, seq_p, C_WIDTH))
        vc_l.append(cv.reshape(nbp, seq_p, C_WIDTH))

        aq, ak, av, zb, cq, ck, cv = _inproj(xs, mod_s, g0, w_ab[l], w_g[l], w_c[l])
        (o_a,) = _attention(aq, ak, av, nb=nbs, seq=seq_s, n_q=A_HEADS, n_kv=A_KV_HEADS, tq=Q_TILE, layer=l,
                            kc=ck_a, vc=cv_a, gq=gq, gk=gk, cos=cos, sin=sin)
        o_b, _ = _deltanet(zb, state_b, conv_w[l], a_log[l], dt_bias[l], g_onorm_b[l], nb=nbs, seq=seq_s, layer=l)
        o_c = _nbr_attention(cq, ck, cv, ck_c, cv_c, tt, nb=nbs, seq=seq_s, layer=l)
        xs = _post(xs, o_a, o_b, o_c, mod_s, g_norm[l], goa, goc, w_out_b[l], w_gu_b[l], w_down_b[l])

    def stack_heads(parts, heads):
        return jnp.stack(parts, axis=1).reshape(nbp, depth, seq_p, heads, HEAD_DIM)

    return (xp.reshape(nbp, seq_p, d), xs.reshape(nbs, seq_s, d),
            stack_heads(ka_l, A_KV_HEADS), stack_heads(va_l, A_KV_HEADS), jnp.stack(sb_l, axis=1),
            stack_heads(kc_l, C_HEADS), stack_heads(vc_l, C_HEADS))
```

```python
import functools

import numpy as np
import jax
import jax.numpy as jnp
from jax import lax
from jax.experimental import pallas as pl
from jax.experimental.pallas import tpu as pltpu

F32 = jnp.float32
BF16 = jnp.bfloat16

D_MODEL = 1024
HEAD_DIM = 64
GRID_W = 64
A_HEADS, A_KV_HEADS, B_HEADS, C_HEADS = 6, 2, 4, 6
A_WIDTH, A_KV_WIDTH = A_HEADS * HEAD_DIM, A_KV_HEADS * HEAD_DIM
B_WIDTH, C_WIDTH = B_HEADS * HEAD_DIM, C_HEADS * HEAD_DIM
N_DIR = 2
CHUNK = 64
WIN_R, WIN_C = 8, 16
ROPE_THETA = 10000.0
D_FF = 2816
N_MOD = 6
EPS = 1e-6
LANES = 128
ZA_W = A_WIDTH + 2 * A_KV_WIDTH
ZB_W = 3 * B_WIDTH + B_WIDTH + LANES
ZC_W = 3 * C_WIDTH
GATE_OFF = 4 * B_WIDTH
VMEM_LIMIT = 56 * 1024 * 1024
TOKEN_TILE = 512
Q_TILE = 256
DELTA_UNROLL = 4
NBR_G = 4
NBR_KROWS = 12
NEG_INF = float("-inf")


def _cparams(sem):
    return pltpu.CompilerParams(dimension_semantics=sem, vmem_limit_bytes=VMEM_LIMIT)


def _split2(x):
    hi = x.astype(BF16)
    lo = (x - hi.astype(F32)).astype(BF16)
    return hi, lo


def _split3(x):
    hi = x.astype(BF16)
    r1 = x - hi.astype(F32)
    mid = r1.astype(BF16)
    lo = (r1 - mid.astype(F32)).astype(BF16)
    return hi, mid, lo


def _dot(a, b):
    return jnp.dot(a, b, preferred_element_type=F32)


def _nt(a, b):
    return lax.dot_general(a, b, (((1,), (1,)), ((), ())), preferred_element_type=F32)


def _tn(a, b):
    return lax.dot_general(a, b, (((0,), (0,)), ((), ())), preferred_element_type=F32)


def _head_ones(width):
    r = lax.broadcasted_iota(jnp.int32, (width, width), 0) // HEAD_DIM
    c = lax.broadcasted_iota(jnp.int32, (width, width), 1) // HEAD_DIM
    return jnp.where(r == c, 1.0, 0.0).astype(BF16)


def _head_sum(x, ones):
    hi, lo = _split2(x)
    return _dot(hi, ones) + _dot(lo, ones)


def _head_rmsnorm(x, g_row, ones):
    ms = _head_sum(x * x, ones) * (1.0 / HEAD_DIM)
    return x * lax.rsqrt(ms + EPS) * g_row


def _rmsnorm(x, g_row):
    ms = jnp.mean(x * x, axis=-1, keepdims=True)
    return x * lax.rsqrt(ms + EPS) * g_row


def _silu(x):
    return x * jax.nn.sigmoid(x)


def _rope(x, cos, sin_signed, width):
    reps = width // LANES
    if reps > 1:
        cos = jnp.concatenate([cos] * reps, axis=1)
        sin_signed = jnp.concatenate([sin_signed] * reps, axis=1)
    lane = lax.broadcasted_iota(jnp.int32, x.shape, 1)
    first = (lane & (HEAD_DIM // 2)) == 0
    partner = jnp.where(first, pltpu.roll(x, width - HEAD_DIM // 2, 1), pltpu.roll(x, HEAD_DIM // 2, 1))
    return x * cos + partner * sin_signed


def _mod_kernel(c_ref, w_ref, b_ref, o_ref):
    a = _silu(c_ref[...]).astype(BF16)
    o_ref[0] = _dot(a, w_ref[0].astype(BF16)) + b_ref[0]


def _modulation(cvec8, w_mod, b_mod):
    depth, d, n = w_mod.shape
    tn = 1536
    return pl.pallas_call(
        _mod_kernel,
        out_shape=jax.ShapeDtypeStruct((depth, 8, n), F32),
        grid=(depth, n // tn),
        in_specs=[pl.BlockSpec((8, d), lambda l, j: (0, 0)),
                  pl.BlockSpec((1, d, tn), lambda l, j: (l, 0, j)),
                  pl.BlockSpec((1, 1, tn), lambda l, j: (l, 0, j))],
        out_specs=pl.BlockSpec((1, 8, tn), lambda l, j: (l, 0, j)),
        compiler_params=_cparams(("arbitrary", "arbitrary")),
        name="modulation",
    )(cvec8, w_mod, b_mod.reshape(depth, 1, n))


INPROJ_OUT_W = (A_WIDTH, A_KV_WIDTH, A_KV_WIDTH, ZB_W, C_WIDTH, C_WIDTH, C_WIDTH)


def _inproj_kernel(*refs, n_caches):
    x_ref, mod_ref, g_ref, wab_ref, wg_ref, wc_ref = refs[:6]
    aq_ref, ak_ref, av_ref, zb_ref, cq_ref, ck_ref, cv_ref = refs[6 + n_caches:]
    x = x_ref[...]
    h = (_rmsnorm(x, g_ref[...]) * (1.0 + mod_ref[0, 1:2, :]) + mod_ref[0, 0:1, :]).astype(BF16)
    z = _dot(h, wab_ref[...])
    aq_ref[...] = z[:, 0:A_WIDTH]
    ak_ref[...] = z[:, A_WIDTH:A_WIDTH + A_KV_WIDTH]
    av_ref[...] = z[:, A_WIDTH + A_KV_WIDTH:ZA_W].reshape(av_ref.shape)
    zb_ref[:, 0:GATE_OFF] = z[:, ZA_W:]
    zb_ref[:, GATE_OFF:] = _dot(h, wg_ref[...])
    zc = _dot(h, wc_ref[...])
    cq_ref[...] = zc[:, 0:C_WIDTH]
    ck_ref[...] = zc[:, C_WIDTH:2 * C_WIDTH].reshape(ck_ref.shape)
    cv_ref[...] = zc[:, 2 * C_WIDTH:].reshape(cv_ref.shape)


def _inproj(x, mod, g_row, w_ab, w_g, w_c, layer, caches=None):
    m, d = x.shape
    tm = TOKEN_TILE
    nb = mod.shape[0]
    per = (m // nb) // tm
    const = lambda i: (0, 0)
    wspec = lambda w: pl.BlockSpec((None,) + w.shape[1:], lambda i: (layer, 0, 0), pipeline_mode=pl.Buffered(1))
    out_shape = [jax.ShapeDtypeStruct((m, w), F32) for w in INPROJ_OUT_W]
    out_specs = [pl.BlockSpec((tm, w), lambda i: (i, 0)) for w in INPROJ_OUT_W]
    args = [x, mod, g_row, w_ab, w_g, w_c]
    in_specs = [pl.BlockSpec((tm, d), lambda i: (i, 0)),
                pl.BlockSpec((1, N_MOD, d), lambda i: (i // per, 0, 0)),
                pl.BlockSpec((1, d), const),
                wspec(w_ab), wspec(w_g), wspec(w_c)]
    aliases = {}
    if caches is not None:
        seq = caches[0].shape[2]
        for slot, arr in zip((2, 5, 6), caches):
            aliases[len(args)] = slot
            args.append(arr)
            in_specs.append(pl.BlockSpec(memory_space=pl.ANY))
            out_shape[slot] = jax.ShapeDtypeStruct(arr.shape, F32)
            out_specs[slot] = pl.BlockSpec((tm // seq, None, seq, arr.shape[3]), lambda i: (i, layer, 0, 0))
    return pl.pallas_call(
        functools.partial(_inproj_kernel, n_caches=len(aliases)),
        out_shape=tuple(out_shape),
        grid=(m // tm,),
        in_specs=in_specs,
        out_specs=tuple(out_specs),
        input_output_aliases=aliases,
        compiler_params=_cparams(("arbitrary",)),
        name="inproj",
    )(*args)


def _attn_kernel(*refs, n_q, n_kv, tq, l_self, l_ctx, norm, rope, emit_k):
    it = iter(refs)
    q_ref, k_ref, v_ref = next(it), next(it), next(it)
    kc_ref = vc_ref = gq_ref = gk_ref = cos_ref = sin_ref = kout_ref = None
    if l_ctx:
        kc_ref, vc_ref = next(it), next(it)
    if norm:
        gq_ref, gk_ref = next(it), next(it)
    if rope:
        cos_ref, sin_ref = next(it), next(it)
    if emit_k:
        next(it)
    o_ref = next(it)
    if emit_k:
        kout_ref = next(it)
    kbuf, vbuf = next(it), next(it)

    j = pl.program_id(1)
    wq, wk = n_q * HEAD_DIM, n_kv * HEAD_DIM
    grp = n_q // n_kv
    rb = 256

    @pl.when(j == 0)
    def _():
        ones_k = _head_ones(wk) if norm else None

        def body(i, carry):
            r0 = pl.multiple_of(i * rb, rb)
            k = k_ref[pl.ds(r0, rb), :]
            if norm:
                k = _head_rmsnorm(k, gk_ref[...], ones_k)
            if rope:
                k = _rope(k, cos_ref[pl.ds(r0, rb), :], sin_ref[pl.ds(r0, rb), :], wk)
            if emit_k:
                kout_ref[pl.ds(r0, rb), :] = k
            kbuf[pl.ds(r0, rb), :] = k.astype(BF16)
            vbuf[pl.ds(r0, rb), :] = v_ref[pl.ds(r0, rb), :].astype(BF16)
            return carry
        lax.fori_loop(0, l_self // rb, body, 0)
        if l_ctx:
            kbuf[l_self:l_self + l_ctx, :] = kc_ref[...].astype(BF16)
            vbuf[l_self:l_self + l_ctx, :] = vc_ref[...].astype(BF16)

    q = q_ref[...]
    if norm:
        q = _head_rmsnorm(q, gq_ref[...], _head_ones(wq))
    if rope:
        r0 = pl.multiple_of(j * tq, tq)
        q = _rope(q, cos_ref[pl.ds(r0, tq), :], sin_ref[pl.ds(r0, tq), :], wq)
    q = (q * (HEAD_DIM ** -0.5)).astype(BF16)
    for h in range(n_q):
        kv = h // grp
        qh = q[:, h * HEAD_DIM:(h + 1) * HEAD_DIM]
        s = _nt(qh, kbuf[:, kv * HEAD_DIM:(kv + 1) * HEAD_DIM])
        m = jnp.max(s, axis=-1, keepdims=True)
        p = jnp.exp(s - m)
        l = jnp.sum(p, axis=-1, keepdims=True)
        o = _dot(p.astype(BF16), vbuf[:, kv * HEAD_DIM:(kv + 1) * HEAD_DIM])
        o_ref[:, h * HEAD_DIM:(h + 1) * HEAD_DIM] = o / l


def _attention(q, k, v, *, nb, seq, n_q, n_kv, tq, layer=None, kc=None, vc=None, gq=None, gk=None,
               cos=None, sin=None, k_cache=None):
    wq, wk = n_q * HEAD_DIM, n_kv * HEAD_DIM
    nq = seq // tq
    l_ctx = 0 if kc is None else kc.shape[2]
    norm, rope = gq is not None, cos is not None
    emit_k = k_cache is not None
    stacked = pl.BlockSpec((None, None, seq, wk), lambda b, j: (b, layer, 0, 0))
    flat = pl.BlockSpec((seq, wk), lambda b, j: (b, 0))
    args = [q, k, v]
    in_specs = [pl.BlockSpec((tq, wq), lambda b, j: (b * nq + j, 0)),
                stacked if k.ndim == 4 else flat, stacked if v.ndim == 4 else flat]
    if l_ctx:
        args += [kc, vc]
        in_specs += [pl.BlockSpec((None, None, l_ctx, wk), lambda b, j: (b, layer, 0, 0))] * 2
    if norm:
        args += [gq, gk]
        in_specs += [pl.BlockSpec((1, wq), lambda b, j: (0, 0)), pl.BlockSpec((1, wk), lambda b, j: (0, 0))]
    if rope:
        args += [cos, sin]
        in_specs += [pl.BlockSpec((seq, LANES), lambda b, j: (0, 0))] * 2
    out_shape = [jax.ShapeDtypeStruct((nb * seq, wq), F32)]
    out_specs = [pl.BlockSpec((tq, wq), lambda b, j: (b * nq + j, 0))]
    aliases = {}
    if emit_k:
        aliases[len(args)] = 1
        args.append(k_cache)
        in_specs.append(pl.BlockSpec(memory_space=pl.ANY))
        out_shape.append(jax.ShapeDtypeStruct(k_cache.shape, F32))
        out_specs.append(stacked)
    kern = functools.partial(_attn_kernel, n_q=n_q, n_kv=n_kv, tq=tq, l_self=seq, l_ctx=l_ctx,
                             norm=norm, rope=rope, emit_k=emit_k)
    return pl.pallas_call(
        kern,
        out_shape=tuple(out_shape),
        grid=(nb, nq),
        in_specs=in_specs,
        out_specs=tuple(out_specs),
        input_output_aliases=aliases,
        scratch_shapes=[pltpu.VMEM((seq + l_ctx, wk), BF16), pltpu.VMEM((seq + l_ctx, wk), BF16)],
        compiler_params=_cparams(("arbitrary", "arbitrary")),
        name="attention",
    )(*args)


def _nbr_plan(rows):
    nblk = rows // NBR_G
    kstart, pairs, ent = [], {}, []
    for blk in range(nblk):
        r0 = blk * NBR_G
        ks = min(max(r0 - WIN_R // 2, 0), rows - NBR_KROWS)
        kstart.append(ks)
        for i in range(NBR_G):
            r = r0 + i
            rs = min(max(r - WIN_R // 2, 0), rows - WIN_R)
            assert ks <= rs and rs + WIN_R <= ks + NBR_KROWS
            for p in range(NBR_KROWS // 2):
                ab = []
                for krow in (ks + 2 * p, ks + 2 * p + 1):
                    valid = rs <= krow < rs + WIN_R
                    ab.append(krow - r + WIN_R if valid else 0)
                ent.append(pairs.setdefault(tuple(ab), len(pairs)))
    pair_list = sorted(pairs, key=pairs.get)
    return (np.asarray(kstart, np.int32), np.asarray(ent, np.int32),
            np.asarray([a for a, _ in pair_list], np.int32), np.asarray([b for _, b in pair_list], np.int32))


def _nbr_bias_tables(rpb, rows):
    _, _, left, right = _nbr_plan(rows)
    col = np.arange(GRID_W)
    c_start = np.clip(col - WIN_C // 2, 0, GRID_W - WIN_C)
    cmask = (col[None, :] >= c_start[:, None]) & (col[None, :] < c_start[:, None] + WIN_C)
    dc = np.clip(col[None, :] - col[:, None] + (WIN_C - 1), 0, 2 * WIN_C - 2)
    onehot = (dc.reshape(-1)[None, :] == np.arange(2 * WIN_C - 1)[:, None]).astype(np.float32)
    t = jnp.einsum("lhde,ek->lhdk", rpb.astype(F32), jnp.asarray(onehot), precision=lax.Precision.HIGHEST)
    t = t.reshape(rpb.shape[:3] + (GRID_W, GRID_W))
    t = jnp.where(jnp.asarray(cmask), t, NEG_INF)
    masked = jnp.full(t.shape[:2] + (1, GRID_W, GRID_W), NEG_INF, F32)
    tpad = jnp.concatenate([masked, t], axis=2)
    return jnp.concatenate([jnp.take(tpad, jnp.asarray(left), axis=2),
                            jnp.take(tpad, jnp.asarray(right), axis=2)], axis=-1)


def _nbr_kernel(ks_ref, ent_ref, q_ref, k_ref, v_ref, kc_ref, vc_ref, tt_ref, o_ref,
                kbuf, vbuf, kcbuf, vcbuf, s_ref, *, seq):
    blk = pl.program_id(1)
    nk = NBR_KROWS * GRID_W
    npair = NBR_KROWS // 2
    rb = 256

    @pl.when(blk == 0)
    def _():
        def body(i, carry):
            r0 = pl.multiple_of(i * rb, rb)
            kbuf[pl.ds(r0, rb), :] = k_ref[pl.ds(r0, rb), :].astype(BF16)
            vbuf[pl.ds(r0, rb), :] = v_ref[pl.ds(r0, rb), :].astype(BF16)
            return carry
        lax.fori_loop(0, seq // rb, body, 0)
        kcbuf[...] = kc_ref[...].astype(BF16)
        vcbuf[...] = vc_ref[...].astype(BF16)

    ks = pl.multiple_of(ks_ref[blk] * GRID_W, GRID_W)
    q = (q_ref[...] * (HEAD_DIM ** -0.5)).astype(BF16)
    for h in range(C_HEADS):
        hs = slice(h * HEAD_DIM, (h + 1) * HEAD_DIM)
        qh = q[:, hs]
        s_ref[...] = _nt(qh, kbuf[pl.ds(ks, nk), hs])
        for i in range(NBR_G):
            for p in range(npair):
                e = ent_ref[(blk * NBR_G + i) * npair + p]
                rs_, cs_ = slice(i * GRID_W, (i + 1) * GRID_W), slice(p * LANES, (p + 1) * LANES)
                s_ref[rs_, cs_] = s_ref[rs_, cs_] + tt_ref[h, e]
        s_loc = s_ref[...]
        s_ctx = _nt(qh, kcbuf[:, hs])
        m = jnp.maximum(jnp.max(s_loc, axis=-1, keepdims=True), jnp.max(s_ctx, axis=-1, keepdims=True))
        p_loc = jnp.exp(s_loc - m)
        p_ctx = jnp.exp(s_ctx - m)
        l = jnp.sum(p_loc, axis=-1, keepdims=True) + jnp.sum(p_ctx, axis=-1, keepdims=True)
        o = _dot(p_loc.astype(BF16), vbuf[pl.ds(ks, nk), hs]) + _dot(p_ctx.astype(BF16), vcbuf[:, hs])
        o_ref[:, hs] = o / l


def _nbr_attention(q, k, v, kc, vc, tt, *, nb, seq, layer):
    rows = seq // GRID_W
    kstart, ent, _, _ = _nbr_plan(rows)
    nblk = rows // NBR_G
    tq = NBR_G * GRID_W
    nk = NBR_KROWS * GRID_W
    l_ctx = kc.shape[2]
    n_ent = tt.shape[2]
    smem = pl.BlockSpec(memory_space=pltpu.SMEM)
    return pl.pallas_call(
        functools.partial(_nbr_kernel, seq=seq),
        out_shape=jax.ShapeDtypeStruct((nb * seq, C_WIDTH), F32),
        grid=(nb, nblk),
        in_specs=[smem, smem,
                  pl.BlockSpec((tq, C_WIDTH), lambda b, j: (b * nblk + j, 0)),
                  pl.BlockSpec((seq, C_WIDTH), lambda b, j: (b, 0)),
                  pl.BlockSpec((seq, C_WIDTH), lambda b, j: (b, 0)),
                  pl.BlockSpec((None, None, l_ctx, C_WIDTH), lambda b, j: (b, layer, 0, 0)),
                  pl.BlockSpec((None, None, l_ctx, C_WIDTH), lambda b, j: (b, layer, 0, 0)),
                  pl.BlockSpec((None, C_HEADS, n_ent, GRID_W, LANES), lambda b, j: (layer, 0, 0, 0, 0))],
        out_specs=pl.BlockSpec((tq, C_WIDTH), lambda b, j: (b * nblk + j, 0)),
        scratch_shapes=[pltpu.VMEM((seq, C_WIDTH), BF16), pltpu.VMEM((seq, C_WIDTH), BF16),
                        pltpu.VMEM((l_ctx, C_WIDTH), BF16), pltpu.VMEM((l_ctx, C_WIDTH), BF16),
                        pltpu.VMEM((tq, nk), F32)],
        compiler_params=_cparams(("arbitrary", "arbitrary")),
        name="nbr_attention",
    )(jnp.asarray(kstart), jnp.asarray(ent), q, k, v, kc, vc, tt)


def _bd(x, mask):
    return jnp.concatenate([x] * B_HEADS, axis=0) * mask[jnp.dtype(x.dtype).name]


def _unpack_bd(full, lane_masks):
    out = None
    for g in range(B_HEADS):
        blk = full[g * HEAD_DIM:(g + 1) * HEAD_DIM, :] * lane_masks[g]
        out = blk if out is None else out + blk
    return out


def _pk_mm(a, b, mask):
    return _dot(a.astype(BF16), _bd(b.astype(BF16), mask))


def _pk_unit_tri_inverses(lmats, lowers, ri, cj, mask):
    eye = jnp.where(ri == cj, 1.0, 0.0).astype(F32)
    ts = [eye] * len(lmats)
    s = 1
    while s < CHUNK:
        same_pair = ((ri ^ cj) & ~(2 * s - 1)) == 0
        r_hi, c_hi = (ri & s) != 0, (cj & s) != 0
        sel = {True: jnp.where(same_pair & r_hi & ~c_hi, 1.0, 0.0), False: jnp.where(same_pair & ~r_hi & c_hi, 1.0, 0.0)}
        offs = [lm * sel[lo] for lm, lo in zip(lmats, lowers)]
        if s == 1:
            ts = [t - off for t, off in zip(ts, offs)]
        else:
            mids = [_pk_mm(t, off, mask) for t, off in zip(ts, offs)]
            ts = [t - _pk_mm(m, t, mask) for t, m in zip(ts, mids)]
        s *= 2
    return ts


def _delta_kernel(*refs, seq, has_s0, has_cache):
    it = iter(refs)
    zb_ref = next(it)
    s0_ref = next(it) if has_s0 else None
    convw_ref, alog_ref, dtb_ref, gon_ref = next(it), next(it), next(it), next(it)
    if has_cache:
        next(it)
    o_ref, sfin_ref = next(it), next(it)
    qt_s, au_s, p_s, n_s, gam_s, of_s, ob_s, st_s = (next(it) for _ in range(8))

    n = seq // CHUNK
    c64 = CHUNK
    pw = B_WIDTH
    row = lax.broadcasted_iota(jnp.int32, (c64, 1), 0)
    lane = lax.broadcasted_iota(jnp.int32, (c64, LANES), 1)
    ri = lax.broadcasted_iota(jnp.int32, (c64, pw), 0)
    lane_p = lax.broadcasted_iota(jnp.int32, (c64, pw), 1)
    cj = lane_p & (HEAD_DIM - 1)
    lane_head = [jnp.where(lane_p // HEAD_DIM == g, 1.0, 0.0) for g in range(B_HEADS)]
    off_diag = jnp.where(ri == cj, 0.0, 1.0)
    bd_bool = (lax.broadcasted_iota(jnp.int32, (pw, pw), 0) // HEAD_DIM
               == lax.broadcasted_iota(jnp.int32, (pw, pw), 1) // HEAD_DIM)
    bd_mask = {"float32": jnp.where(bd_bool, 1.0, 0.0), "bfloat16": jnp.where(bd_bool, 1.0, 0.0).astype(BF16)}
    ones_b = _head_ones(pw)
    w0, w1, w2 = convw_ref[0:1, :], convw_ref[1:2, :], convw_ref[2:3, :]
    neg_decay_rate = -jnp.exp(alog_ref[...])
    eye8 = jnp.where(lax.broadcasted_iota(jnp.int32, (8, LANES), 0) == lax.broadcasted_iota(jnp.int32, (8, LANES), 1),
                     1.0, 0.0).astype(BF16)
    expand = jnp.where(lax.broadcasted_iota(jnp.int32, (LANES, N_DIR * pw), 1) // HEAD_DIM
                       == lax.broadcasted_iota(jnp.int32, (LANES, N_DIR * pw), 0), 1.0, 0.0).astype(BF16)

    def chunk_terms(c):
        r0 = pl.multiple_of(c * c64, c64)
        x = zb_ref[pl.ds(r0, c64), 0:3 * B_WIDTH]
        prev = zb_ref[pl.ds(jnp.maximum(r0 - 1, 0), 1), 0:3 * B_WIDTH] * jnp.where(c > 0, 1.0, 0.0)
        nxt = zb_ref[pl.ds(jnp.minimum(r0 + c64, seq - 1), 1), 0:3 * B_WIDTH] * jnp.where(c < n - 1, 1.0, 0.0)
        x_up = jnp.where(row == 0, prev, pltpu.roll(x, 1, 0))
        x_dn = jnp.where(row == c64 - 1, nxt, pltpu.roll(x, c64 - 1, 0))
        y = _silu(x_up * w0 + x * w1 + x_dn * w2)
        q, k, v = y[:, 0:pw], y[:, pw:2 * pw], y[:, 2 * pw:]
        q = q * lax.rsqrt(_head_sum(q * q, ones_b) + EPS) * (HEAD_DIM ** -0.5)
        k = k * lax.rsqrt(_head_sum(k * k, ones_b) + EPS)

        ga = zb_ref[pl.ds(r0, c64), GATE_OFF:GATE_OFF + LANES]
        beta = jax.nn.sigmoid(ga)
        a = pltpu.roll(ga, LANES - N_DIR * B_HEADS, 1) + dtb_ref[...]
        g = neg_decay_rate * (jnp.maximum(a, 0.0) + jnp.log(1.0 + jnp.exp(-jnp.abs(a))))
        cf, cr = g, g
        sh = 1
        while sh < c64:
            cf = cf + jnp.where(row >= sh, pltpu.roll(cf, sh, 0), 0.0)
            cr = cr + jnp.where(row < c64 - sh, pltpu.roll(cr, c64 - sh, 0), 0.0)
            sh *= 2
        gc = jnp.where(lane < B_HEADS, cf, cr)
        g3 = _split3(gc)
        gct = _nt(eye8, g3[0]) + (_nt(eye8, g3[1]) + _nt(eye8, g3[2]))
        b2 = _split2(beta)
        ex = _dot(jnp.concatenate([b2[0], b2[1], g3[0], g3[1], g3[2]], axis=0), expand)
        bexp = ex[0:c64] + ex[c64:2 * c64]
        gexp = ex[2 * c64:3 * c64] + (ex[3 * c64:4 * c64] + ex[4 * c64:])

        k_bd = _bd(k.astype(BF16), bd_mask)
        q_bf = q.astype(BF16)
        probs = []
        for d in range(N_DIR):
            ls = slice(d * pw, (d + 1) * pw)
            bx, gx = bexp[:, ls], gexp[:, ls]
            incl = (ri >= cj) if d == 0 else (ri <= cj)
            last = c64 - 1 if d == 0 else 0
            g_row = jnp.concatenate([gct[d * B_HEADS + h:d * B_HEADS + h + 1, :] for h in range(B_HEADS)], axis=1)
            g_last = gx[last:last + 1, :]
            decay = jnp.exp(jnp.where(incl, gx - g_row, NEG_INF))
            kb = k * bx
            aa = _nt(jnp.concatenate([kb.astype(BF16), q_bf], axis=0), k_bd)
            eg = jnp.exp(gx)
            probs.append(dict(
                c=c, d=d, lmat=aa[0:c64] * (decay * off_diag),
                a_intra=(aa[c64:] * decay).astype(BF16),
                vb=v * bx, kbe=kb * eg, qe=q * eg, k_dec=(k * jnp.exp(g_last - gx)).astype(BF16),
                gam=jnp.exp(g_last)))
        return probs

    def terms_loop(i, carry):
        probs = [p for j in range(DELTA_UNROLL) for p in chunk_terms(i * DELTA_UNROLL + j)]
        tinvs = _pk_unit_tri_inverses([p["lmat"] for p in probs], [p["d"] == 0 for p in probs], ri, cj, bd_mask)
        us = [_pk_mm(t, p["vb"], bd_mask).astype(BF16) for t, p in zip(tinvs, probs)]
        ws = [_pk_mm(t, p["kbe"], bd_mask).astype(BF16) for t, p in zip(tinvs, probs)]
        for p, u, w in zip(probs, us, ws):
            c, d = p["c"], p["d"]
            qt_s[c, d] = (p["qe"] - _dot(p["a_intra"], _bd(w, bd_mask))).astype(BF16)
            au_s[c, d] = _dot(p["a_intra"], _bd(u, bd_mask))
            full = _tn(p["k_dec"], jnp.concatenate([w, u], axis=1))
            p_s[c, d] = _unpack_bd(full[:, 0:pw], lane_head).astype(BF16)
            n_s[c, d] = _unpack_bd(full[:, pw:], lane_head)
            gam_s[c, d] = jnp.broadcast_to(p["gam"], (8, pw))
        return carry

    lax.fori_loop(0, n // DELTA_UNROLL, terms_loop, 0)

    for d in range(N_DIR):
        if has_s0:
            s_pk = jnp.concatenate([s0_ref[d, h] for h in range(B_HEADS)], axis=1)
            st_s[d] = _bd(s_pk, bd_mask)
        else:
            st_s[d] = jnp.zeros((pw, pw), F32)

    def scan(t, carry):
        for d in range(N_DIR):
            c = t if d == 0 else n - 1 - t
            r0 = pl.multiple_of(c * c64, c64)
            s = st_s[d]
            s_bf = s.astype(BF16)
            out_s = of_s if d == 0 else ob_s
            out_s[pl.ds(r0, c64), :] = _dot(qt_s[c, d], s_bf) + au_s[c, d]
            st_s[d] = (s * gam_s[c, d, 0:1, :] - _dot(_bd(p_s[c, d], bd_mask), s_bf)) + _bd(n_s[c, d], bd_mask)
        return carry

    lax.fori_loop(0, n, scan, 0)

    fb = DELTA_UNROLL * c64

    def finish(i, carry):
        r0 = pl.multiple_of(i * fb, fb)
        o = of_s[pl.ds(r0, fb), :] + ob_s[pl.ds(r0, fb), :]
        y = _head_rmsnorm(o, gon_ref[...], ones_b)
        o_ref[pl.ds(r0, fb), :] = y * _silu(zb_ref[pl.ds(r0, fb), 3 * B_WIDTH:4 * B_WIDTH])
        return carry

    lax.fori_loop(0, n // DELTA_UNROLL, finish, 0)
    for d in range(N_DIR):
        s_pk = _unpack_bd(st_s[d], lane_head)
        for h in range(B_HEADS):
            sfin_ref[d, h] = s_pk[:, h * HEAD_DIM:(h + 1) * HEAD_DIM]


def _deltanet(zb, s0, conv_w, a_log, dt_bias, g_onorm, *, nb, seq, layer=None, s_cache=None):
    n = seq // CHUNK
    has_s0 = s0 is not None
    pad = LANES - N_DIR * B_HEADS
    alog_row = jnp.pad(a_log.reshape(1, -1).astype(F32), ((0, 0), (0, pad)))
    dtb_row = jnp.pad(dt_bias.reshape(1, -1).astype(F32), ((0, 0), (0, pad)))
    gon_row = jnp.tile(g_onorm.astype(F32), B_HEADS).reshape(1, B_WIDTH)
    st_shape = (N_DIR, B_HEADS, HEAD_DIM, HEAD_DIM)
    args = [zb]
    in_specs = [pl.BlockSpec((seq, ZB_W), lambda b: (b, 0))]
    if has_s0:
        args.append(s0)
        in_specs.append(pl.BlockSpec((None, None) + st_shape, lambda b: (b, layer, 0, 0, 0, 0)))
    args += [conv_w.astype(F32), alog_row, dtb_row, gon_row]
    in_specs += [pl.BlockSpec((3, 3 * B_WIDTH), lambda b: (0, 0)),
                 pl.BlockSpec((1, LANES), lambda b: (0, 0)),
                 pl.BlockSpec((1, LANES), lambda b: (0, 0)),
                 pl.BlockSpec((1, B_WIDTH), lambda b: (0, 0))]
    per_chunk = (n, N_DIR, CHUNK, B_WIDTH)
    aliases = {}
    if s_cache is None:
        st_out = jax.ShapeDtypeStruct((nb,) + st_shape, F32)
        st_spec = pl.BlockSpec((None,) + st_shape, lambda b: (b, 0, 0, 0, 0))
    else:
        aliases[len(args)] = 1
        args.append(s_cache)
        in_specs.append(pl.BlockSpec(memory_space=pl.ANY))
        st_out = jax.ShapeDtypeStruct(s_cache.shape, F32)
        st_spec = pl.BlockSpec((None, None) + st_shape, lambda b: (b, layer, 0, 0, 0, 0))
    return pl.pallas_call(
        functools.partial(_delta_kernel, seq=seq, has_s0=has_s0, has_cache=s_cache is not None),
        out_shape=(jax.ShapeDtypeStruct((nb * seq, B_WIDTH), F32), st_out),
        grid=(nb,),
        in_specs=in_specs,
        out_specs=(pl.BlockSpec((seq, B_WIDTH), lambda b: (b, 0)), st_spec),
        input_output_aliases=aliases,
        scratch_shapes=[pltpu.VMEM(per_chunk, BF16), pltpu.VMEM(per_chunk, F32),
                        pltpu.VMEM(per_chunk, BF16), pltpu.VMEM(per_chunk, F32),
                        pltpu.VMEM((n, N_DIR, 8, B_WIDTH), F32),
                        pltpu.VMEM((seq, B_WIDTH), F32), pltpu.VMEM((seq, B_WIDTH), F32),
                        pltpu.VMEM((N_DIR, B_WIDTH, B_WIDTH), F32)],
        compiler_params=_cparams(("arbitrary",)),
        name="deltanet",
    )(*args)


def _post_kernel(x_ref, oa_ref, ob_ref, oc_ref, mod_ref, gn_ref, ga_ref, gc_ref,
                 wo_ref, wgu_ref, wd_ref, y_ref):
    x = x_ref[...]
    ya = _rmsnorm(oa_ref[...], ga_ref[...])
    yc = _rmsnorm(oc_ref[...], gc_ref[...])
    y = jnp.concatenate([ya, ob_ref[...], yc], axis=-1).astype(BF16)
    m = _dot(y, wo_ref[...])
    x1 = x + mod_ref[0, 2:3, :] * _rmsnorm(m, gn_ref[1:2, :])
    h = _rmsnorm(x1, gn_ref[2:3, :]) * (1.0 + mod_ref[0, 4:5, :]) + mod_ref[0, 3:4, :]
    gu = _dot(h.astype(BF16), wgu_ref[...])
    act = (_silu(gu[:, 0:D_FF]) * gu[:, D_FF:]).astype(BF16)
    f = _dot(act, wd_ref[...])
    y_ref[...] = x1 + mod_ref[0, 5:6, :] * _rmsnorm(f, gn_ref[3:4, :])


def _post(x, oa, ob, oc, mod, g_norm, g_out_a, g_out_c, w_out, w_gu, w_down, layer):
    m, d = x.shape
    tm = TOKEN_TILE
    nb = mod.shape[0]
    per = (m // nb) // tm
    row = lambda i: (i, 0)
    const = lambda i: (0, 0)
    wspec = lambda w: pl.BlockSpec((None,) + w.shape[1:], lambda i: (layer, 0, 0), pipeline_mode=pl.Buffered(1))
    return pl.pallas_call(
        _post_kernel,
        out_shape=jax.ShapeDtypeStruct((m, d), F32),
        grid=(m // tm,),
        in_specs=[pl.BlockSpec((tm, d), row),
                  pl.BlockSpec((tm, A_WIDTH), row),
                  pl.BlockSpec((tm, B_WIDTH), row),
                  pl.BlockSpec((tm, C_WIDTH), row),
                  pl.BlockSpec((1, N_MOD, d), lambda i: (i // per, 0, 0)),
                  pl.BlockSpec((4, d), const),
                  pl.BlockSpec((1, A_WIDTH), const),
                  pl.BlockSpec((1, C_WIDTH), const),
                  wspec(w_out), wspec(w_gu), wspec(w_down)],
        out_specs=pl.BlockSpec((tm, d), row),
        compiler_params=_cparams(("arbitrary",)),
        name="post",
    )(x, oa, ob, oc, mod, g_norm, g_out_a, g_out_c, w_out, w_gu, w_down)


def _rope_tables(seq):
    quarter = HEAD_DIM // 4
    inv = ROPE_THETA ** (-jnp.arange(quarter, dtype=F32) / quarter)
    t = jnp.arange(seq)
    rowp = (t // GRID_W).astype(F32)
    colp = (t % GRID_W).astype(F32)
    ang = jnp.concatenate([rowp[:, None] * inv, colp[:, None] * inv], axis=-1)
    cos, sin = jnp.cos(ang), jnp.sin(ang)
    reps = LANES // HEAD_DIM
    return (jnp.tile(jnp.concatenate([cos, cos], axis=-1), (1, reps)),
            jnp.tile(jnp.concatenate([-sin, sin], axis=-1), (1, reps)))


def _split_w_in(w_in):
    n_gate = 2 * N_DIR * B_HEADS
    cut = ZA_W + 4 * B_WIDTH
    w_g = jnp.pad(w_in[..., cut:cut + n_gate], ((0, 0), (0, 0), (0, LANES - n_gate)))
    return w_in[..., :cut].astype(BF16), w_g.astype(BF16), w_in[..., cut + n_gate:].astype(BF16)


def kernel(x_prompt, x_sample, cache_a_k, cache_a_v, state_b, cache_c_k, cache_c_v, c, c_ctx, w_mod, b_mod,
           g_norm, w_in, g_qk_a, g_out_a, conv_w, a_log, dt_bias, g_onorm_b, rpb, g_out_c, w_out, w_gu, w_down):
    nbp, seq_p, d = x_prompt.shape
    nbs, seq_s, _ = x_sample.shape
    depth = w_mod.shape[0]
    past = cache_a_k.shape[2]

    cvec = jnp.concatenate([c_ctx[None, :], c, jnp.zeros((8 - 1 - nbs, d), F32)], axis=0)
    mods = _modulation(cvec, w_mod, b_mod)
    w_ab, w_g, w_c = _split_w_in(w_in)
    w_out_b, w_gu_b, w_down_b = w_out.astype(BF16), w_gu.astype(BF16), w_down.astype(BF16)
    cos, sin = _rope_tables(seq_s)
    tt = _nbr_bias_tables(rpb, seq_s // GRID_W)
    ck_a = cache_a_k.reshape(nbs, depth, past, A_KV_WIDTH)
    cv_a = cache_a_v.reshape(nbs, depth, past, A_KV_WIDTH)
    ck_c = cache_c_k.reshape(nbs, depth, past, C_WIDTH)
    cv_c = cache_c_v.reshape(nbs, depth, past, C_WIDTH)

    xp = x_prompt.reshape(nbp * seq_p, d)
    xs = x_sample.reshape(nbs * seq_s, d)
    ka_st = jnp.zeros((nbp, depth, seq_p, A_KV_WIDTH), F32)
    va_st = jnp.zeros((nbp, depth, seq_p, A_KV_WIDTH), F32)
    kc_st = jnp.zeros((nbp, depth, seq_p, C_WIDTH), F32)
    vc_st = jnp.zeros((nbp, depth, seq_p, C_WIDTH), F32)
    sb_st = jnp.zeros((nbp, depth, N_DIR, B_HEADS, HEAD_DIM, HEAD_DIM), F32)
    for l in range(depth):
        gq = jnp.tile(g_qk_a[l, 0], A_HEADS).reshape(1, A_WIDTH)
        gk = jnp.tile(g_qk_a[l, 1], A_KV_HEADS).reshape(1, A_KV_WIDTH)
        goa, goc = g_out_a[l].reshape(1, A_WIDTH), g_out_c[l].reshape(1, C_WIDTH)
        g0 = g_norm[l, 0:1]
        mod_p = mods[l, 0:1].reshape(1, N_MOD, d)
        mod_s = mods[l, 1:1 + nbs].reshape(nbs, N_MOD, d)

        aq, ak, va_st, zb, cq, kc_st, vc_st = _inproj(xp, mod_p, g0, w_ab, w_g, w_c, l, caches=(va_st, kc_st, vc_st))
        o_a, ka_st = _attention(aq, ak, va_st, nb=nbp, seq=seq_p, n_q=A_HEADS, n_kv=A_KV_HEADS, tq=seq_p, layer=l,
                                gq=gq, gk=gk, k_cache=ka_st)
        o_b, sb_st = _deltanet(zb, None, conv_w[l], a_log[l], dt_bias[l], g_onorm_b[l], nb=nbp, seq=seq_p,
                               layer=l, s_cache=sb_st)
        (o_c,) = _attention(cq, kc_st, vc_st, nb=nbp, seq=seq_p, n_q=C_HEADS, n_kv=C_HEADS, tq=seq_p, layer=l)
        xp = _post(xp, o_a, o_b, o_c, mod_p, g_norm[l], goa, goc, w_out_b, w_gu_b, w_down_b, l)

        aq, ak, av, zb, cq, ck, cv = _inproj(xs, mod_s, g0, w_ab, w_g, w_c, l)
        (o_a,) = _attention(aq, ak, av, nb=nbs, seq=seq_s, n_q=A_HEADS, n_kv=A_KV_HEADS, tq=Q_TILE, layer=l,
                            kc=ck_a, vc=cv_a, gq=gq, gk=gk, cos=cos, sin=sin)
        o_b, _ = _deltanet(zb, state_b, conv_w[l], a_log[l], dt_bias[l], g_onorm_b[l], nb=nbs, seq=seq_s, layer=l)
        o_c = _nbr_attention(cq, ck, cv, ck_c, cv_c, tt, nb=nbs, seq=seq_s, layer=l)
        xs = _post(xs, o_a, o_b, o_c, mod_s, g_norm[l], goa, goc, w_out_b, w_gu_b, w_down_b, l)

    heads = lambda st, n_heads: st.reshape(nbp, depth, seq_p, n_heads, HEAD_DIM)
    return (xp.reshape(nbp, seq_p, d), xs.reshape(nbs, seq_s, d),
            heads(ka_st, A_KV_HEADS), heads(va_st, A_KV_HEADS), sb_st, heads(kc_st, C_HEADS), heads(vc_st, C_HEADS))
```

```python
import functools

import numpy as np
import jax
import jax.numpy as jnp
from jax import lax
from jax.experimental import pallas as pl
from jax.experimental.pallas import tpu as pltpu

F32 = jnp.float32
BF16 = jnp.bfloat16

D_MODEL = 1024
HEAD_DIM = 64
GRID_W = 64
A_HEADS, A_KV_HEADS, B_HEADS, C_HEADS = 6, 2, 4, 6
A_WIDTH, A_KV_WIDTH = A_HEADS * HEAD_DIM, A_KV_HEADS * HEAD_DIM
B_WIDTH, C_WIDTH = B_HEADS * HEAD_DIM, C_HEADS * HEAD_DIM
N_DIR = 2
CHUNK = 64
WIN_R, WIN_C = 8, 16
ROPE_THETA = 10000.0
D_FF = 2816
N_MOD = 6
EPS = 1e-6
LANES = 128
ZA_W = A_WIDTH + 2 * A_KV_WIDTH
ZB_W = 3 * B_WIDTH + B_WIDTH + LANES
ZC_W = 3 * C_WIDTH
GATE_OFF = 4 * B_WIDTH
VMEM_LIMIT = 56 * 1024 * 1024
TOKEN_TILE = 512
Q_TILE = 1024
DELTA_UNROLL = 4
NBR_G = 4
NBR_KROWS = 12
NEG_INF = float("-inf")


def _cparams(sem):
    return pltpu.CompilerParams(dimension_semantics=sem, vmem_limit_bytes=VMEM_LIMIT)


def _split2(x):
    hi = x.astype(BF16)
    lo = (x - hi.astype(F32)).astype(BF16)
    return hi, lo


def _split3(x):
    hi = x.astype(BF16)
    r1 = x - hi.astype(F32)
    mid = r1.astype(BF16)
    lo = (r1 - mid.astype(F32)).astype(BF16)
    return hi, mid, lo


def _dot(a, b):
    return jnp.dot(a, b, preferred_element_type=F32)


def _nt(a, b):
    return lax.dot_general(a, b, (((1,), (1,)), ((), ())), preferred_element_type=F32)


def _tn(a, b):
    return lax.dot_general(a, b, (((0,), (0,)), ((), ())), preferred_element_type=F32)


def _head_ones(width):
    r = lax.broadcasted_iota(jnp.int32, (width, width), 0) // HEAD_DIM
    c = lax.broadcasted_iota(jnp.int32, (width, width), 1) // HEAD_DIM
    return jnp.where(r == c, 1.0, 0.0).astype(BF16)


def _head_sum(x, ones):
    hi, lo = _split2(x)
    return _dot(hi, ones) + _dot(lo, ones)


def _head_rmsnorm(x, g_row, ones):
    ms = _head_sum(x * x, ones) * (1.0 / HEAD_DIM)
    return x * lax.rsqrt(ms + EPS) * g_row


def _rmsnorm(x, g_row):
    ms = jnp.mean(x * x, axis=-1, keepdims=True)
    return x * lax.rsqrt(ms + EPS) * g_row


def _silu(x):
    return x * jax.nn.sigmoid(x)


def _rope(x, cos, sin_signed, width):
    reps = width // LANES
    if reps > 1:
        cos = jnp.concatenate([cos] * reps, axis=1)
        sin_signed = jnp.concatenate([sin_signed] * reps, axis=1)
    lane = lax.broadcasted_iota(jnp.int32, x.shape, 1)
    first = (lane & (HEAD_DIM // 2)) == 0
    partner = jnp.where(first, pltpu.roll(x, width - HEAD_DIM // 2, 1), pltpu.roll(x, HEAD_DIM // 2, 1))
    return x * cos + partner * sin_signed


def _mod_kernel(c_ref, w_ref, b_ref, o_ref):
    a = _silu(c_ref[...]).astype(BF16)
    o_ref[0] = _dot(a, w_ref[0].astype(BF16)) + b_ref[0]


def _modulation(cvec8, w_mod, b_mod):
    depth, d, n = w_mod.shape
    tn = 1536
    return pl.pallas_call(
        _mod_kernel,
        out_shape=jax.ShapeDtypeStruct((depth, 8, n), F32),
        grid=(depth, n // tn),
        in_specs=[pl.BlockSpec((8, d), lambda l, j: (0, 0)),
                  pl.BlockSpec((1, d, tn), lambda l, j: (l, 0, j)),
                  pl.BlockSpec((1, 1, tn), lambda l, j: (l, 0, j))],
        out_specs=pl.BlockSpec((1, 8, tn), lambda l, j: (l, 0, j)),
        compiler_params=_cparams(("arbitrary", "arbitrary")),
        name="modulation",
    )(cvec8, w_mod, b_mod.reshape(depth, 1, n))


INPROJ_OUT_W = (A_WIDTH, A_KV_WIDTH, A_KV_WIDTH, ZB_W, C_WIDTH, C_WIDTH, C_WIDTH)


def _inproj_kernel(*refs, n_caches):
    x_ref, mod_ref, g_ref, wab_ref, wg_ref, wc_ref = refs[:6]
    aq_ref, ak_ref, av_ref, zb_ref, cq_ref, ck_ref, cv_ref = refs[6 + n_caches:]
    x = x_ref[...]
    h = (_rmsnorm(x, g_ref[...]) * (1.0 + mod_ref[0, 1:2, :]) + mod_ref[0, 0:1, :]).astype(BF16)
    z = _dot(h, wab_ref[...])
    aq_ref[...] = z[:, 0:A_WIDTH]
    ak_ref[...] = z[:, A_WIDTH:A_WIDTH + A_KV_WIDTH]
    av_ref[...] = z[:, A_WIDTH + A_KV_WIDTH:ZA_W].reshape(av_ref.shape)
    zb_ref[:, 0:GATE_OFF] = z[:, ZA_W:]
    zb_ref[:, GATE_OFF:] = _dot(h, wg_ref[...])
    zc = _dot(h, wc_ref[...])
    cq_ref[...] = zc[:, 0:C_WIDTH]
    ck_ref[...] = zc[:, C_WIDTH:2 * C_WIDTH].reshape(ck_ref.shape)
    cv_ref[...] = zc[:, 2 * C_WIDTH:].reshape(cv_ref.shape)


def _inproj(x, mod, g_row, w_ab, w_g, w_c, layer, caches=None):
    m, d = x.shape
    tm = TOKEN_TILE
    nb = mod.shape[0]
    per = (m // nb) // tm
    const = lambda i: (0, 0)
    wspec = lambda w: pl.BlockSpec((None,) + w.shape[1:], lambda i: (layer, 0, 0), pipeline_mode=pl.Buffered(1))
    out_shape = [jax.ShapeDtypeStruct((m, w), F32) for w in INPROJ_OUT_W]
    out_specs = [pl.BlockSpec((tm, w), lambda i: (i, 0)) for w in INPROJ_OUT_W]
    args = [x, mod, g_row, w_ab, w_g, w_c]
    in_specs = [pl.BlockSpec((tm, d), lambda i: (i, 0)),
                pl.BlockSpec((1, N_MOD, d), lambda i: (i // per, 0, 0)),
                pl.BlockSpec((1, d), const),
                wspec(w_ab), wspec(w_g), wspec(w_c)]
    aliases = {}
    if caches is not None:
        seq = caches[0].shape[2]
        for slot, arr in zip((2, 5, 6), caches):
            aliases[len(args)] = slot
            args.append(arr)
            in_specs.append(pl.BlockSpec(memory_space=pl.ANY))
            out_shape[slot] = jax.ShapeDtypeStruct(arr.shape, F32)
            out_specs[slot] = pl.BlockSpec((tm // seq, None, seq, arr.shape[3]), lambda i: (i, layer, 0, 0))
    return pl.pallas_call(
        functools.partial(_inproj_kernel, n_caches=len(aliases)),
        out_shape=tuple(out_shape),
        grid=(m // tm,),
        in_specs=in_specs,
        out_specs=tuple(out_specs),
        input_output_aliases=aliases,
        compiler_params=_cparams(("arbitrary",)),
        name="inproj",
    )(*args)


def _attn_kernel(*refs, n_q, n_kv, tq, l_self, l_ctx, norm, rope, emit_k):
    it = iter(refs)
    q_ref, k_ref, v_ref = next(it), next(it), next(it)
    kc_ref = vc_ref = gq_ref = gk_ref = cos_ref = sin_ref = kout_ref = None
    if l_ctx:
        kc_ref, vc_ref = next(it), next(it)
    if norm:
        gq_ref, gk_ref = next(it), next(it)
    if rope:
        cos_ref, sin_ref = next(it), next(it)
    if emit_k:
        next(it)
    o_ref = next(it)
    if emit_k:
        kout_ref = next(it)
    kbuf, vbuf = next(it), next(it)

    j = pl.program_id(1)
    wq, wk = n_q * HEAD_DIM, n_kv * HEAD_DIM
    grp = n_q // n_kv
    rb = 256

    @pl.when(j == 0)
    def _():
        ones_k = _head_ones(wk) if norm else None

        def body(i, carry):
            r0 = pl.multiple_of(i * rb, rb)
            k = k_ref[pl.ds(r0, rb), :]
            if norm:
                k = _head_rmsnorm(k, gk_ref[...], ones_k)
            if rope:
                k = _rope(k, cos_ref[pl.ds(r0, rb), :], sin_ref[pl.ds(r0, rb), :], wk)
            if emit_k:
                kout_ref[pl.ds(r0, rb), :] = k
            kbuf[pl.ds(r0, rb), :] = k.astype(BF16)
            vbuf[pl.ds(r0, rb), :] = v_ref[pl.ds(r0, rb), :].astype(BF16)
            return carry
        lax.fori_loop(0, l_self // rb, body, 0)
        if l_ctx:
            kbuf[l_self:l_self + l_ctx, :] = kc_ref[...].astype(BF16)
            vbuf[l_self:l_self + l_ctx, :] = vc_ref[...].astype(BF16)

    q = q_ref[...]
    if norm:
        q = _head_rmsnorm(q, gq_ref[...], _head_ones(wq))
    if rope:
        r0 = pl.multiple_of(j * tq, tq)
        q = _rope(q, cos_ref[pl.ds(r0, tq), :], sin_ref[pl.ds(r0, tq), :], wq)
    q = (q * (HEAD_DIM ** -0.5)).astype(BF16)
    for h in range(n_q):
        kv = h // grp
        qh = q[:, h * HEAD_DIM:(h + 1) * HEAD_DIM]
        s = _nt(qh, kbuf[:, kv * HEAD_DIM:(kv + 1) * HEAD_DIM])
        m = jnp.max(s, axis=-1, keepdims=True)
        p = jnp.exp(s - m)
        l = jnp.sum(p, axis=-1, keepdims=True)
        o = _dot(p.astype(BF16), vbuf[:, kv * HEAD_DIM:(kv + 1) * HEAD_DIM])
        o_ref[:, h * HEAD_DIM:(h + 1) * HEAD_DIM] = o / l


def _attention(q, k, v, *, nb, seq, n_q, n_kv, tq, layer=None, kc=None, vc=None, gq=None, gk=None,
               cos=None, sin=None, k_cache=None):
    wq, wk = n_q * HEAD_DIM, n_kv * HEAD_DIM
    nq = seq // tq
    l_ctx = 0 if kc is None else kc.shape[2]
    norm, rope = gq is not None, cos is not None
    emit_k = k_cache is not None
    stacked = pl.BlockSpec((None, None, seq, wk), lambda b, j: (b, layer, 0, 0))
    flat = pl.BlockSpec((seq, wk), lambda b, j: (b, 0))
    args = [q, k, v]
    in_specs = [pl.BlockSpec((tq, wq), lambda b, j: (b * nq + j, 0)),
                stacked if k.ndim == 4 else flat, stacked if v.ndim == 4 else flat]
    if l_ctx:
        args += [kc, vc]
        in_specs += [pl.BlockSpec((None, None, l_ctx, wk), lambda b, j: (b, layer, 0, 0))] * 2
    if norm:
        args += [gq, gk]
        in_specs += [pl.BlockSpec((1, wq), lambda b, j: (0, 0)), pl.BlockSpec((1, wk), lambda b, j: (0, 0))]
    if rope:
        args += [cos, sin]
        in_specs += [pl.BlockSpec((seq, LANES), lambda b, j: (0, 0))] * 2
    out_shape = [jax.ShapeDtypeStruct((nb * seq, wq), F32)]
    out_specs = [pl.BlockSpec((tq, wq), lambda b, j: (b * nq + j, 0))]
    aliases = {}
    if emit_k:
        aliases[len(args)] = 1
        args.append(k_cache)
        in_specs.append(pl.BlockSpec(memory_space=pl.ANY))
        out_shape.append(jax.ShapeDtypeStruct(k_cache.shape, F32))
        out_specs.append(stacked)
    kern = functools.partial(_attn_kernel, n_q=n_q, n_kv=n_kv, tq=tq, l_self=seq, l_ctx=l_ctx,
                             norm=norm, rope=rope, emit_k=emit_k)
    return pl.pallas_call(
        kern,
        out_shape=tuple(out_shape),
        grid=(nb, nq),
        in_specs=in_specs,
        out_specs=tuple(out_specs),
        input_output_aliases=aliases,
        scratch_shapes=[pltpu.VMEM((seq + l_ctx, wk), BF16), pltpu.VMEM((seq + l_ctx, wk), BF16)],
        compiler_params=_cparams(("arbitrary", "arbitrary")),
        name="attention",
    )(*args)


def _nbr_plan(rows):
    nblk = rows // NBR_G
    kstart, pairs, ent = [], {}, []
    for blk in range(nblk):
        r0 = blk * NBR_G
        ks = min(max(r0 - WIN_R // 2, 0), rows - NBR_KROWS)
        kstart.append(ks)
        for i in range(NBR_G):
            r = r0 + i
            rs = min(max(r - WIN_R // 2, 0), rows - WIN_R)
            assert ks <= rs and rs + WIN_R <= ks + NBR_KROWS
            for p in range(NBR_KROWS // 2):
                ab = []
                for krow in (ks + 2 * p, ks + 2 * p + 1):
                    valid = rs <= krow < rs + WIN_R
                    ab.append(krow - r + WIN_R if valid else 0)
                ent.append(pairs.setdefault(tuple(ab), len(pairs)))
    pair_list = sorted(pairs, key=pairs.get)
    return (np.asarray(kstart, np.int32), np.asarray(ent, np.int32),
            np.asarray([a for a, _ in pair_list], np.int32), np.asarray([b for _, b in pair_list], np.int32))


def _nbr_bias_tables(rpb, rows):
    _, _, left, right = _nbr_plan(rows)
    col = np.arange(GRID_W)
    c_start = np.clip(col - WIN_C // 2, 0, GRID_W - WIN_C)
    cmask = (col[None, :] >= c_start[:, None]) & (col[None, :] < c_start[:, None] + WIN_C)
    dc = np.clip(col[None, :] - col[:, None] + (WIN_C - 1), 0, 2 * WIN_C - 2)
    onehot = (dc.reshape(-1)[None, :] == np.arange(2 * WIN_C - 1)[:, None]).astype(np.float32)
    t = jnp.einsum("lhde,ek->lhdk", rpb.astype(F32), jnp.asarray(onehot), precision=lax.Precision.HIGHEST)
    t = t.reshape(rpb.shape[:3] + (GRID_W, GRID_W))
    t = jnp.where(jnp.asarray(cmask), t, NEG_INF)
    masked = jnp.full(t.shape[:2] + (1, GRID_W, GRID_W), NEG_INF, F32)
    tpad = jnp.concatenate([masked, t], axis=2)
    return jnp.concatenate([jnp.take(tpad, jnp.asarray(left), axis=2),
                            jnp.take(tpad, jnp.asarray(right), axis=2)], axis=-1)


def _nbr_kernel(ks_ref, ent_ref, q_ref, k_ref, v_ref, kc_ref, vc_ref, tt_ref, o_ref,
                kbuf, vbuf, kcbuf, vcbuf, s_ref, *, seq):
    blk = pl.program_id(1)
    nk = NBR_KROWS * GRID_W
    npair = NBR_KROWS // 2
    rb = 256

    @pl.when(blk == 0)
    def _():
        def body(i, carry):
            r0 = pl.multiple_of(i * rb, rb)
            kbuf[pl.ds(r0, rb), :] = k_ref[pl.ds(r0, rb), :].astype(BF16)
            vbuf[pl.ds(r0, rb), :] = v_ref[pl.ds(r0, rb), :].astype(BF16)
            return carry
        lax.fori_loop(0, seq // rb, body, 0)
        kcbuf[...] = kc_ref[...].astype(BF16)
        vcbuf[...] = vc_ref[...].astype(BF16)

    ks = pl.multiple_of(ks_ref[blk] * GRID_W, GRID_W)
    q = (q_ref[...] * (HEAD_DIM ** -0.5)).astype(BF16)
    for h in range(C_HEADS):
        hs = slice(h * HEAD_DIM, (h + 1) * HEAD_DIM)
        qh = q[:, hs]
        s_ref[...] = _nt(qh, kbuf[pl.ds(ks, nk), hs])
        for i in range(NBR_G):
            for p in range(npair):
                e = ent_ref[(blk * NBR_G + i) * npair + p]
                rs_, cs_ = slice(i * GRID_W, (i + 1) * GRID_W), slice(p * LANES, (p + 1) * LANES)
                s_ref[rs_, cs_] = s_ref[rs_, cs_] + tt_ref[h, e]
        s_loc = s_ref[...]
        s_ctx = _nt(qh, kcbuf[:, hs])
        m = jnp.maximum(jnp.max(s_loc, axis=-1, keepdims=True), jnp.max(s_ctx, axis=-1, keepdims=True))
        p_loc = jnp.exp(s_loc - m)
        p_ctx = jnp.exp(s_ctx - m)
        l = jnp.sum(p_loc, axis=-1, keepdims=True) + jnp.sum(p_ctx, axis=-1, keepdims=True)
        o = _dot(p_loc.astype(BF16), vbuf[pl.ds(ks, nk), hs]) + _dot(p_ctx.astype(BF16), vcbuf[:, hs])
        o_ref[:, hs] = o / l


def _nbr_attention(q, k, v, kc, vc, tt, *, nb, seq, layer):
    rows = seq // GRID_W
    kstart, ent, _, _ = _nbr_plan(rows)
    nblk = rows // NBR_G
    tq = NBR_G * GRID_W
    nk = NBR_KROWS * GRID_W
    l_ctx = kc.shape[2]
    n_ent = tt.shape[2]
    smem = pl.BlockSpec(memory_space=pltpu.SMEM)
    return pl.pallas_call(
        functools.partial(_nbr_kernel, seq=seq),
        out_shape=jax.ShapeDtypeStruct((nb * seq, C_WIDTH), F32),
        grid=(nb, nblk),
        in_specs=[smem, smem,
                  pl.BlockSpec((tq, C_WIDTH), lambda b, j: (b * nblk + j, 0)),
                  pl.BlockSpec((seq, C_WIDTH), lambda b, j: (b, 0)),
                  pl.BlockSpec((seq, C_WIDTH), lambda b, j: (b, 0)),
                  pl.BlockSpec((None, None, l_ctx, C_WIDTH), lambda b, j: (b, layer, 0, 0)),
                  pl.BlockSpec((None, None, l_ctx, C_WIDTH), lambda b, j: (b, layer, 0, 0)),
                  pl.BlockSpec((None, C_HEADS, n_ent, GRID_W, LANES), lambda b, j: (layer, 0, 0, 0, 0))],
        out_specs=pl.BlockSpec((tq, C_WIDTH), lambda b, j: (b * nblk + j, 0)),
        scratch_shapes=[pltpu.VMEM((seq, C_WIDTH), BF16), pltpu.VMEM((seq, C_WIDTH), BF16),
                        pltpu.VMEM((l_ctx, C_WIDTH), BF16), pltpu.VMEM((l_ctx, C_WIDTH), BF16),
                        pltpu.VMEM((tq, nk), F32)],
        compiler_params=_cparams(("arbitrary", "arbitrary")),
        name="nbr_attention",
    )(jnp.asarray(kstart), jnp.asarray(ent), q, k, v, kc, vc, tt)


def _bd(x, mask):
    return jnp.concatenate([x] * B_HEADS, axis=0) * mask[jnp.dtype(x.dtype).name]


def _unpack_bd(full, lane_masks):
    out = None
    for g in range(B_HEADS):
        blk = full[g * HEAD_DIM:(g + 1) * HEAD_DIM, :] * lane_masks[g]
        out = blk if out is None else out + blk
    return out


def _pk_mm(a, b, mask):
    return _dot(a.astype(BF16), _bd(b.astype(BF16), mask))


def _pk_unit_tri_inverses(lmats, lowers, ri, cj, mask):
    eye = jnp.where(ri == cj, 1.0, 0.0).astype(F32)
    ts = [eye] * len(lmats)
    s = 1
    while s < CHUNK:
        same_pair = ((ri ^ cj) & ~(2 * s - 1)) == 0
        r_hi, c_hi = (ri & s) != 0, (cj & s) != 0
        sel = {True: jnp.where(same_pair & r_hi & ~c_hi, 1.0, 0.0), False: jnp.where(same_pair & ~r_hi & c_hi, 1.0, 0.0)}
        offs = [lm * sel[lo] for lm, lo in zip(lmats, lowers)]
        if s == 1:
            ts = [t - off for t, off in zip(ts, offs)]
        else:
            mids = [_pk_mm(t, off, mask) for t, off in zip(ts, offs)]
            ts = [t - _pk_mm(m, t, mask) for t, m in zip(ts, mids)]
        s *= 2
    return ts


def _delta_kernel(*refs, seq, has_s0, has_cache):
    it = iter(refs)
    zb_ref = next(it)
    s0_ref = next(it) if has_s0 else None
    convw_ref, alog_ref, dtb_ref, gon_ref = next(it), next(it), next(it), next(it)
    if has_cache:
        next(it)
    o_ref, sfin_ref = next(it), next(it)
    qt_s, au_s, p_s, n_s, gam_s, of_s, ob_s, st_s = (next(it) for _ in range(8))

    n = seq // CHUNK
    c64 = CHUNK
    pw = B_WIDTH
    row = lax.broadcasted_iota(jnp.int32, (c64, 1), 0)
    lane = lax.broadcasted_iota(jnp.int32, (c64, LANES), 1)
    ri = lax.broadcasted_iota(jnp.int32, (c64, pw), 0)
    lane_p = lax.broadcasted_iota(jnp.int32, (c64, pw), 1)
    cj = lane_p & (HEAD_DIM - 1)
    lane_head = [jnp.where(lane_p // HEAD_DIM == g, 1.0, 0.0) for g in range(B_HEADS)]
    off_diag = jnp.where(ri == cj, 0.0, 1.0)
    bd_bool = (lax.broadcasted_iota(jnp.int32, (pw, pw), 0) // HEAD_DIM
               == lax.broadcasted_iota(jnp.int32, (pw, pw), 1) // HEAD_DIM)
    bd_mask = {"float32": jnp.where(bd_bool, 1.0, 0.0), "bfloat16": jnp.where(bd_bool, 1.0, 0.0).astype(BF16)}
    ones_b = _head_ones(pw)
    w0, w1, w2 = convw_ref[0:1, :], convw_ref[1:2, :], convw_ref[2:3, :]
    neg_decay_rate = -jnp.exp(alog_ref[...])
    eye8 = jnp.where(lax.broadcasted_iota(jnp.int32, (8, LANES), 0) == lax.broadcasted_iota(jnp.int32, (8, LANES), 1),
                     1.0, 0.0).astype(BF16)
    expand = jnp.where(lax.broadcasted_iota(jnp.int32, (LANES, N_DIR * pw), 1) // HEAD_DIM
                       == lax.broadcasted_iota(jnp.int32, (LANES, N_DIR * pw), 0), 1.0, 0.0).astype(BF16)

    def chunk_terms(c):
        r0 = pl.multiple_of(c * c64, c64)
        x = zb_ref[pl.ds(r0, c64), 0:3 * B_WIDTH]
        prev = zb_ref[pl.ds(jnp.maximum(r0 - 1, 0), 1), 0:3 * B_WIDTH] * jnp.where(c > 0, 1.0, 0.0)
        nxt = zb_ref[pl.ds(jnp.minimum(r0 + c64, seq - 1), 1), 0:3 * B_WIDTH] * jnp.where(c < n - 1, 1.0, 0.0)
        x_up = jnp.where(row == 0, prev, pltpu.roll(x, 1, 0))
        x_dn = jnp.where(row == c64 - 1, nxt, pltpu.roll(x, c64 - 1, 0))
        y = _silu(x_up * w0 + x * w1 + x_dn * w2)
        q, k, v = y[:, 0:pw], y[:, pw:2 * pw], y[:, 2 * pw:]
        q = q * lax.rsqrt(_head_sum(q * q, ones_b) + EPS) * (HEAD_DIM ** -0.5)
        k = k * lax.rsqrt(_head_sum(k * k, ones_b) + EPS)

        ga = zb_ref[pl.ds(r0, c64), GATE_OFF:GATE_OFF + LANES]
        beta = jax.nn.sigmoid(ga)
        a = pltpu.roll(ga, LANES - N_DIR * B_HEADS, 1) + dtb_ref[...]
        g = neg_decay_rate * (jnp.maximum(a, 0.0) + jnp.log(1.0 + jnp.exp(-jnp.abs(a))))
        cf, cr = g, g
        sh = 1
        while sh < c64:
            cf = cf + jnp.where(row >= sh, pltpu.roll(cf, sh, 0), 0.0)
            cr = cr + jnp.where(row < c64 - sh, pltpu.roll(cr, c64 - sh, 0), 0.0)
            sh *= 2
        gc = jnp.where(lane < B_HEADS, cf, cr)
        g3 = _split3(gc)
        gct = _nt(eye8, g3[0]) + (_nt(eye8, g3[1]) + _nt(eye8, g3[2]))
        b2 = _split2(beta)
        ex = _dot(jnp.concatenate([b2[0], b2[1], g3[0], g3[1], g3[2]], axis=0), expand)
        bexp = ex[0:c64] + ex[c64:2 * c64]
        gexp = ex[2 * c64:3 * c64] + (ex[3 * c64:4 * c64] + ex[4 * c64:])

        k_bd = _bd(k.astype(BF16), bd_mask)
        q_bf = q.astype(BF16)
        probs = []
        for d in range(N_DIR):
            ls = slice(d * pw, (d + 1) * pw)
            bx, gx = bexp[:, ls], gexp[:, ls]
            incl = (ri >= cj) if d == 0 else (ri <= cj)
            last = c64 - 1 if d == 0 else 0
            g_row = jnp.concatenate([gct[d * B_HEADS + h:d * B_HEADS + h + 1, :] for h in range(B_HEADS)], axis=1)
            g_last = gx[last:last + 1, :]
            decay = jnp.exp(jnp.where(incl, gx - g_row, NEG_INF))
            kb = k * bx
            aa = _nt(jnp.concatenate([kb.astype(BF16), q_bf], axis=0), k_bd)
            eg = jnp.exp(gx)
            probs.append(dict(
                c=c, d=d, lmat=aa[0:c64] * (decay * off_diag),
                a_intra=(aa[c64:] * decay).astype(BF16),
                vb=v * bx, kbe=kb * eg, qe=q * eg, k_dec=(k * jnp.exp(g_last - gx)).astype(BF16),
                gam=jnp.exp(g_last)))
        return probs

    def terms_loop(i, carry):
        probs = [p for j in range(DELTA_UNROLL) for p in chunk_terms(i * DELTA_UNROLL + j)]
        tinvs = _pk_unit_tri_inverses([p["lmat"] for p in probs], [p["d"] == 0 for p in probs], ri, cj, bd_mask)
        us = [_pk_mm(t, p["vb"], bd_mask).astype(BF16) for t, p in zip(tinvs, probs)]
        ws = [_pk_mm(t, p["kbe"], bd_mask).astype(BF16) for t, p in zip(tinvs, probs)]
        for p, u, w in zip(probs, us, ws):
            c, d = p["c"], p["d"]
            qt_s[c, d] = (p["qe"] - _dot(p["a_intra"], _bd(w, bd_mask))).astype(BF16)
            au_s[c, d] = _dot(p["a_intra"], _bd(u, bd_mask))
            full = _tn(p["k_dec"], jnp.concatenate([w, u], axis=1))
            p_s[c, d] = _unpack_bd(full[:, 0:pw], lane_head).astype(BF16)
            n_s[c, d] = _unpack_bd(full[:, pw:], lane_head)
            gam_s[c, d] = jnp.broadcast_to(p["gam"], (8, pw))
        return carry

    lax.fori_loop(0, n // DELTA_UNROLL, terms_loop, 0)

    for d in range(N_DIR):
        if has_s0:
            s_pk = jnp.concatenate([s0_ref[d, h] for h in range(B_HEADS)], axis=1)
            st_s[d] = _bd(s_pk, bd_mask)
        else:
            st_s[d] = jnp.zeros((pw, pw), F32)

    def scan(t, carry):
        for d in range(N_DIR):
            c = t if d == 0 else n - 1 - t
            r0 = pl.multiple_of(c * c64, c64)
            s = st_s[d]
            s_bf = s.astype(BF16)
            out_s = of_s if d == 0 else ob_s
            out_s[pl.ds(r0, c64), :] = _dot(qt_s[c, d], s_bf) + au_s[c, d]
            st_s[d] = (s * gam_s[c, d, 0:1, :] - _dot(_bd(p_s[c, d], bd_mask), s_bf)) + _bd(n_s[c, d], bd_mask)
        return carry

    lax.fori_loop(0, n, scan, 0)

    fb = DELTA_UNROLL * c64

    def finish(i, carry):
        r0 = pl.multiple_of(i * fb, fb)
        o = of_s[pl.ds(r0, fb), :] + ob_s[pl.ds(r0, fb), :]
        y = _head_rmsnorm(o, gon_ref[...], ones_b)
        o_ref[pl.ds(r0, fb), :] = y * _silu(zb_ref[pl.ds(r0, fb), 3 * B_WIDTH:4 * B_WIDTH])
        return carry

    lax.fori_loop(0, n // DELTA_UNROLL, finish, 0)
    for d in range(N_DIR):
        s_pk = _unpack_bd(st_s[d], lane_head)
        for h in range(B_HEADS):
            sfin_ref[d, h] = s_pk[:, h * HEAD_DIM:(h + 1) * HEAD_DIM]


def _deltanet(zb, s0, conv_w, a_log, dt_bias, g_onorm, *, nb, seq, layer=None, s_cache=None):
    n = seq // CHUNK
    has_s0 = s0 is not None
    pad = LANES - N_DIR * B_HEADS
    alog_row = jnp.pad(a_log.reshape(1, -1).astype(F32), ((0, 0), (0, pad)))
    dtb_row = jnp.pad(dt_bias.reshape(1, -1).astype(F32), ((0, 0), (0, pad)))
    gon_row = jnp.tile(g_onorm.astype(F32), B_HEADS).reshape(1, B_WIDTH)
    st_shape = (N_DIR, B_HEADS, HEAD_DIM, HEAD_DIM)
    args = [zb]
    in_specs = [pl.BlockSpec((seq, ZB_W), lambda b: (b, 0))]
    if has_s0:
        args.append(s0)
        in_specs.append(pl.BlockSpec((None, None) + st_shape, lambda b: (b, layer, 0, 0, 0, 0)))
    args += [conv_w.astype(F32), alog_row, dtb_row, gon_row]
    in_specs += [pl.BlockSpec((3, 3 * B_WIDTH), lambda b: (0, 0)),
                 pl.BlockSpec((1, LANES), lambda b: (0, 0)),
                 pl.BlockSpec((1, LANES), lambda b: (0, 0)),
                 pl.BlockSpec((1, B_WIDTH), lambda b: (0, 0))]
    per_chunk = (n, N_DIR, CHUNK, B_WIDTH)
    aliases = {}
    if s_cache is None:
        st_out = jax.ShapeDtypeStruct((nb,) + st_shape, F32)
        st_spec = pl.BlockSpec((None,) + st_shape, lambda b: (b, 0, 0, 0, 0))
    else:
        aliases[len(args)] = 1
        args.append(s_cache)
        in_specs.append(pl.BlockSpec(memory_space=pl.ANY))
        st_out = jax.ShapeDtypeStruct(s_cache.shape, F32)
        st_spec = pl.BlockSpec((None, None) + st_shape, lambda b: (b, layer, 0, 0, 0, 0))
    return pl.pallas_call(
        functools.partial(_delta_kernel, seq=seq, has_s0=has_s0, has_cache=s_cache is not None),
        out_shape=(jax.ShapeDtypeStruct((nb * seq, B_WIDTH), F32), st_out),
        grid=(nb,),
        in_specs=in_specs,
        out_specs=(pl.BlockSpec((seq, B_WIDTH), lambda b: (b, 0)), st_spec),
        input_output_aliases=aliases,
        scratch_shapes=[pltpu.VMEM(per_chunk, BF16), pltpu.VMEM(per_chunk, F32),
                        pltpu.VMEM(per_chunk, BF16), pltpu.VMEM(per_chunk, F32),
                        pltpu.VMEM((n, N_DIR, 8, B_WIDTH), F32),
                        pltpu.VMEM((seq, B_WIDTH), F32), pltpu.VMEM((seq, B_WIDTH), F32),
                        pltpu.VMEM((N_DIR, B_WIDTH, B_WIDTH), F32)],
        compiler_params=_cparams(("arbitrary",)),
        name="deltanet",
    )(*args)


def _post_kernel(x_ref, oa_ref, ob_ref, oc_ref, mod_ref, gn_ref, ga_ref, gc_ref,
                 wo_ref, wgu_ref, wd_ref, y_ref):
    x = x_ref[...]
    ya = _rmsnorm(oa_ref[...], ga_ref[...])
    yc = _rmsnorm(oc_ref[...], gc_ref[...])
    y = jnp.concatenate([ya, ob_ref[...], yc], axis=-1).astype(BF16)
    m = _dot(y, wo_ref[...])
    x1 = x + mod_ref[0, 2:3, :] * _rmsnorm(m, gn_ref[1:2, :])
    h = _rmsnorm(x1, gn_ref[2:3, :]) * (1.0 + mod_ref[0, 4:5, :]) + mod_ref[0, 3:4, :]
    gu = _dot(h.astype(BF16), wgu_ref[...])
    act = (_silu(gu[:, 0:D_FF]) * gu[:, D_FF:]).astype(BF16)
    f = _dot(act, wd_ref[...])
    y_ref[...] = x1 + mod_ref[0, 5:6, :] * _rmsnorm(f, gn_ref[3:4, :])


def _post(x, oa, ob, oc, mod, g_norm, g_out_a, g_out_c, w_out, w_gu, w_down, layer):
    m, d = x.shape
    tm = TOKEN_TILE
    nb = mod.shape[0]
    per = (m // nb) // tm
    row = lambda i: (i, 0)
    const = lambda i: (0, 0)
    wspec = lambda w: pl.BlockSpec((None,) + w.shape[1:], lambda i: (layer, 0, 0), pipeline_mode=pl.Buffered(1))
    return pl.pallas_call(
        _post_kernel,
        out_shape=jax.ShapeDtypeStruct((m, d), F32),
        grid=(m // tm,),
        in_specs=[pl.BlockSpec((tm, d), row),
                  pl.BlockSpec((tm, A_WIDTH), row),
                  pl.BlockSpec((tm, B_WIDTH), row),
                  pl.BlockSpec((tm, C_WIDTH), row),
                  pl.BlockSpec((1, N_MOD, d), lambda i: (i // per, 0, 0)),
                  pl.BlockSpec((4, d), const),
                  pl.BlockSpec((1, A_WIDTH), const),
                  pl.BlockSpec((1, C_WIDTH), const),
                  wspec(w_out), wspec(w_gu), wspec(w_down)],
        out_specs=pl.BlockSpec((tm, d), row),
        compiler_params=_cparams(("arbitrary",)),
        name="post",
    )(x, oa, ob, oc, mod, g_norm, g_out_a, g_out_c, w_out, w_gu, w_down)


def _rope_tables(seq):
    quarter = HEAD_DIM // 4
    inv = ROPE_THETA ** (-jnp.arange(quarter, dtype=F32) / quarter)
    t = jnp.arange(seq)
    rowp = (t // GRID_W).astype(F32)
    colp = (t % GRID_W).astype(F32)
    ang = jnp.concatenate([rowp[:, None] * inv, colp[:, None] * inv], axis=-1)
    cos, sin = jnp.cos(ang), jnp.sin(ang)
    reps = LANES // HEAD_DIM
    return (jnp.tile(jnp.concatenate([cos, cos], axis=-1), (1, reps)),
            jnp.tile(jnp.concatenate([-sin, sin], axis=-1), (1, reps)))


def _split_w_in(w_in):
    n_gate = 2 * N_DIR * B_HEADS
    cut = ZA_W + 4 * B_WIDTH
    w_g = jnp.pad(w_in[..., cut:cut + n_gate], ((0, 0), (0, 0), (0, LANES - n_gate)))
    return w_in[..., :cut].astype(BF16), w_g.astype(BF16), w_in[..., cut + n_gate:].astype(BF16)


def kernel(x_prompt, x_sample, cache_a_k, cache_a_v, state_b, cache_c_k, cache_c_v, c, c_ctx, w_mod, b_mod,
           g_norm, w_in, g_qk_a, g_out_a, conv_w, a_log, dt_bias, g_onorm_b, rpb, g_out_c, w_out, w_gu, w_down):
    nbp, seq_p, d = x_prompt.shape
    nbs, seq_s, _ = x_sample.shape
    depth = w_mod.shape[0]
    past = cache_a_k.shape[2]

    cvec = jnp.concatenate([c_ctx[None, :], c, jnp.zeros((8 - 1 - nbs, d), F32)], axis=0)
    mods = _modulation(cvec, w_mod, b_mod)
    w_ab, w_g, w_c = _split_w_in(w_in)
    w_out_b, w_gu_b, w_down_b = w_out.astype(BF16), w_gu.astype(BF16), w_down.astype(BF16)
    cos, sin = _rope_tables(seq_s)
    tt = _nbr_bias_tables(rpb, seq_s // GRID_W)
    ck_a = cache_a_k.reshape(nbs, depth, past, A_KV_WIDTH)
    cv_a = cache_a_v.reshape(nbs, depth, past, A_KV_WIDTH)
    ck_c = cache_c_k.reshape(nbs, depth, past, C_WIDTH)
    cv_c = cache_c_v.reshape(nbs, depth, past, C_WIDTH)

    xp = x_prompt.reshape(nbp * seq_p, d)
    xs = x_sample.reshape(nbs * seq_s, d)
    ka_st = jnp.zeros((nbp, depth, seq_p, A_KV_WIDTH), F32)
    va_st = jnp.zeros((nbp, depth, seq_p, A_KV_WIDTH), F32)
    kc_st = jnp.zeros((nbp, depth, seq_p, C_WIDTH), F32)
    vc_st = jnp.zeros((nbp, depth, seq_p, C_WIDTH), F32)
    sb_st = jnp.zeros((nbp, depth, N_DIR, B_HEADS, HEAD_DIM, HEAD_DIM), F32)
    for l in range(depth):
        gq = jnp.tile(g_qk_a[l, 0], A_HEADS).reshape(1, A_WIDTH)
        gk = jnp.tile(g_qk_a[l, 1], A_KV_HEADS).reshape(1, A_KV_WIDTH)
        goa, goc = g_out_a[l].reshape(1, A_WIDTH), g_out_c[l].reshape(1, C_WIDTH)
        g0 = g_norm[l, 0:1]
        mod_p = mods[l, 0:1].reshape(1, N_MOD, d)
        mod_s = mods[l, 1:1 + nbs].reshape(nbs, N_MOD, d)

        aq, ak, va_st, zb, cq, kc_st, vc_st = _inproj(xp, mod_p, g0, w_ab, w_g, w_c, l, caches=(va_st, kc_st, vc_st))
        o_a, ka_st = _attention(aq, ak, va_st, nb=nbp, seq=seq_p, n_q=A_HEADS, n_kv=A_KV_HEADS, tq=seq_p, layer=l,
                                gq=gq, gk=gk, k_cache=ka_st)
        o_b, sb_st = _deltanet(zb, None, conv_w[l], a_log[l], dt_bias[l], g_onorm_b[l], nb=nbp, seq=seq_p,
                               layer=l, s_cache=sb_st)
        (o_c,) = _attention(cq, kc_st, vc_st, nb=nbp, seq=seq_p, n_q=C_HEADS, n_kv=C_HEADS, tq=seq_p, layer=l)
        xp = _post(xp, o_a, o_b, o_c, mod_p, g_norm[l], goa, goc, w_out_b, w_gu_b, w_down_b, l)

        aq, ak, av, zb, cq, ck, cv = _inproj(xs, mod_s, g0, w_ab, w_g, w_c, l)
        (o_a,) = _attention(aq, ak, av, nb=nbs, seq=seq_s, n_q=A_HEADS, n_kv=A_KV_HEADS, tq=Q_TILE, layer=l,
                            kc=ck_a, vc=cv_a, gq=gq, gk=gk, cos=cos, sin=sin)
        o_b, _ = _deltanet(zb, state_b, conv_w[l], a_log[l], dt_bias[l], g_onorm_b[l], nb=nbs, seq=seq_s, layer=l)
        o_c = _nbr_attention(cq, ck, cv, ck_c, cv_c, tt, nb=nbs, seq=seq_s, layer=l)
        xs = _post(xs, o_a, o_b, o_c, mod_s, g_norm[l], goa, goc, w_out_b, w_gu_b, w_down_b, l)

    heads = lambda st, n_heads: st.reshape(nbp, depth, seq_p, n_heads, HEAD_DIM)
    return (xp.reshape(nbp, seq_p, d), xs.reshape(nbs, seq_s, d),
            heads(ka_st, A_KV_HEADS), heads(va_st, A_KV_HEADS), sb_st, heads(kc_st, C_HEADS), heads(vc_st, C_HEADS))
```

```python
import functools

import numpy as np
import jax
import jax.numpy as jnp
from jax import lax
from jax.experimental import pallas as pl
from jax.experimental.pallas import tpu as pltpu

F32 = jnp.float32
BF16 = jnp.bfloat16

D_MODEL = 1024
HEAD_DIM = 64
GRID_W = 64
A_HEADS, A_KV_HEADS, B_HEADS, C_HEADS = 6, 2, 4, 6
A_WIDTH, A_KV_WIDTH = A_HEADS * HEAD_DIM, A_KV_HEADS * HEAD_DIM
B_WIDTH, C_WIDTH = B_HEADS * HEAD_DIM, C_HEADS * HEAD_DIM
N_DIR = 2
CHUNK = 64
WIN_R, WIN_C = 8, 16
ROPE_THETA = 10000.0
D_FF = 2816
N_MOD = 6
EPS = 1e-6
LANES = 128
ZA_W = A_WIDTH + 2 * A_KV_WIDTH
ZB_W = 3 * B_WIDTH + B_WIDTH + LANES
ZC_W = 3 * C_WIDTH
GATE_OFF = 4 * B_WIDTH
VMEM_LIMIT = 56 * 1024 * 1024
TOKEN_TILE = 512
Q_TILE = 1024
DELTA_UNROLL = 4
NBR_G = 4
NBR_KROWS = 12
NEG_INF = float("-inf")


def _cparams(sem):
    return pltpu.CompilerParams(dimension_semantics=sem, vmem_limit_bytes=VMEM_LIMIT)


def _split2(x):
    hi = x.astype(BF16)
    lo = (x - hi.astype(F32)).astype(BF16)
    return hi, lo


def _split3(x):
    hi = x.astype(BF16)
    r1 = x - hi.astype(F32)
    mid = r1.astype(BF16)
    lo = (r1 - mid.astype(F32)).astype(BF16)
    return hi, mid, lo


def _dot(a, b):
    return jnp.dot(a, b, preferred_element_type=F32)


def _nt(a, b):
    return lax.dot_general(a, b, (((1,), (1,)), ((), ())), preferred_element_type=F32)


def _tn(a, b):
    return lax.dot_general(a, b, (((0,), (0,)), ((), ())), preferred_element_type=F32)


def _head_ones(width):
    r = lax.broadcasted_iota(jnp.int32, (width, width), 0) // HEAD_DIM
    c = lax.broadcasted_iota(jnp.int32, (width, width), 1) // HEAD_DIM
    return jnp.where(r == c, 1.0, 0.0).astype(BF16)


def _head_sum(x, ones):
    hi, lo = _split2(x)
    return _dot(hi, ones) + _dot(lo, ones)


def _head_rmsnorm(x, g_row, ones):
    ms = _head_sum(x * x, ones) * (1.0 / HEAD_DIM)
    return x * lax.rsqrt(ms + EPS) * g_row


def _rmsnorm(x, g_row):
    ms = jnp.mean(x * x, axis=-1, keepdims=True)
    return x * lax.rsqrt(ms + EPS) * g_row


def _silu(x):
    return x * jax.nn.sigmoid(x)


def _rope(x, cos, sin_signed, width):
    reps = width // LANES
    if reps > 1:
        cos = jnp.concatenate([cos] * reps, axis=1)
        sin_signed = jnp.concatenate([sin_signed] * reps, axis=1)
    lane = lax.broadcasted_iota(jnp.int32, x.shape, 1)
    first = (lane & (HEAD_DIM // 2)) == 0
    partner = jnp.where(first, pltpu.roll(x, width - HEAD_DIM // 2, 1), pltpu.roll(x, HEAD_DIM // 2, 1))
    return x * cos + partner * sin_signed


def _mod_kernel(c_ref, w_ref, b_ref, o_ref):
    a = _silu(c_ref[...]).astype(BF16)
    o_ref[0] = _dot(a, w_ref[0].astype(BF16)) + b_ref[0]


def _modulation(cvec8, w_mod, b_mod):
    depth, d, n = w_mod.shape
    tn = 1536
    return pl.pallas_call(
        _mod_kernel,
        out_shape=jax.ShapeDtypeStruct((depth, 8, n), F32),
        grid=(depth, n // tn),
        in_specs=[pl.BlockSpec((8, d), lambda l, j: (0, 0)),
                  pl.BlockSpec((1, d, tn), lambda l, j: (l, 0, j)),
                  pl.BlockSpec((1, 1, tn), lambda l, j: (l, 0, j))],
        out_specs=pl.BlockSpec((1, 8, tn), lambda l, j: (l, 0, j)),
        compiler_params=_cparams(("arbitrary", "arbitrary")),
        name="modulation",
    )(cvec8, w_mod, b_mod.reshape(depth, 1, n))


INPROJ_OUT_W = (A_WIDTH, A_KV_WIDTH, A_KV_WIDTH, ZB_W, C_WIDTH, C_WIDTH, C_WIDTH)


def _inproj_kernel(*refs, n_caches):
    x_ref, mod_ref, g_ref, wab_ref, wg_ref, wc_ref = refs[:6]
    aq_ref, ak_ref, av_ref, zb_ref, cq_ref, ck_ref, cv_ref = refs[6 + n_caches:]
    x = x_ref[...]
    h = (_rmsnorm(x, g_ref[...]) * (1.0 + mod_ref[0, 1:2, :]) + mod_ref[0, 0:1, :]).astype(BF16)
    z = _dot(h, wab_ref[...])
    aq_ref[...] = z[:, 0:A_WIDTH]
    ak_ref[...] = z[:, A_WIDTH:A_WIDTH + A_KV_WIDTH]
    av_ref[...] = z[:, A_WIDTH + A_KV_WIDTH:ZA_W].reshape(av_ref.shape)
    zb_ref[:, 0:GATE_OFF] = z[:, ZA_W:]
    zb_ref[:, GATE_OFF:] = _dot(h, wg_ref[...])
    zc = _dot(h, wc_ref[...])
    cq_ref[...] = zc[:, 0:C_WIDTH]
    ck_ref[...] = zc[:, C_WIDTH:2 * C_WIDTH].reshape(ck_ref.shape)
    cv_ref[...] = zc[:, 2 * C_WIDTH:].reshape(cv_ref.shape)


def _inproj(x, mod, g_row, w_ab, w_g, w_c, layer, caches=None):
    m, d = x.shape
    tm = TOKEN_TILE
    nb = mod.shape[0]
    per = (m // nb) // tm
    const = lambda i: (0, 0)
    wspec = lambda w: pl.BlockSpec((None,) + w.shape[1:], lambda i: (layer, 0, 0), pipeline_mode=pl.Buffered(1))
    out_shape = [jax.ShapeDtypeStruct((m, w), F32) for w in INPROJ_OUT_W]
    out_specs = [pl.BlockSpec((tm, w), lambda i: (i, 0)) for w in INPROJ_OUT_W]
    args = [x, mod, g_row, w_ab, w_g, w_c]
    in_specs = [pl.BlockSpec((tm, d), lambda i: (i, 0)),
                pl.BlockSpec((1, N_MOD, d), lambda i: (i // per, 0, 0)),
                pl.BlockSpec((1, d), const),
                wspec(w_ab), wspec(w_g), wspec(w_c)]
    aliases = {}
    if caches is not None:
        seq = caches[0].shape[2]
        for slot, arr in zip((2, 5, 6), caches):
            aliases[len(args)] = slot
            args.append(arr)
            in_specs.append(pl.BlockSpec(memory_space=pl.ANY))
            out_shape[slot] = jax.ShapeDtypeStruct(arr.shape, F32)
            out_specs[slot] = pl.BlockSpec((tm // seq, None, seq, arr.shape[3]), lambda i: (i, layer, 0, 0))
    return pl.pallas_call(
        functools.partial(_inproj_kernel, n_caches=len(aliases)),
        out_shape=tuple(out_shape),
        grid=(m // tm,),
        in_specs=in_specs,
        out_specs=tuple(out_specs),
        input_output_aliases=aliases,
        compiler_params=_cparams(("arbitrary",)),
        name="inproj",
    )(*args)


def _attn_kernel(*refs, n_q, n_kv, tq, l_self, l_ctx, norm, rope, emit_k):
    it = iter(refs)
    q_ref, k_ref, v_ref = next(it), next(it), next(it)
    kc_ref = vc_ref = gq_ref = gk_ref = cos_ref = sin_ref = kout_ref = None
    if l_ctx:
        kc_ref, vc_ref = next(it), next(it)
    if norm:
        gq_ref, gk_ref = next(it), next(it)
    if rope:
        cos_ref, sin_ref = next(it), next(it)
    if emit_k:
        next(it)
    o_ref = next(it)
    if emit_k:
        kout_ref = next(it)
    kbuf, vbuf = next(it), next(it)

    j = pl.program_id(1)
    wq, wk = n_q * HEAD_DIM, n_kv * HEAD_DIM
    grp = n_q // n_kv
    rb = 256

    @pl.when(j == 0)
    def _():
        ones_k = _head_ones(wk) if norm else None

        def body(i, carry):
            r0 = pl.multiple_of(i * rb, rb)
            k = k_ref[pl.ds(r0, rb), :]
            if norm:
                k = _head_rmsnorm(k, gk_ref[...], ones_k)
            if rope:
                k = _rope(k, cos_ref[pl.ds(r0, rb), :], sin_ref[pl.ds(r0, rb), :], wk)
            if emit_k:
                kout_ref[pl.ds(r0, rb), :] = k
            kbuf[pl.ds(r0, rb), :] = k.astype(BF16)
            vbuf[pl.ds(r0, rb), :] = v_ref[pl.ds(r0, rb), :].astype(BF16)
            return carry
        lax.fori_loop(0, l_self // rb, body, 0)
        if l_ctx:
            kbuf[l_self:l_self + l_ctx, :] = kc_ref[...].astype(BF16)
            vbuf[l_self:l_self + l_ctx, :] = vc_ref[...].astype(BF16)

    q = q_ref[...]
    if norm:
        q = _head_rmsnorm(q, gq_ref[...], _head_ones(wq))
    if rope:
        r0 = pl.multiple_of(j * tq, tq)
        q = _rope(q, cos_ref[pl.ds(r0, tq), :], sin_ref[pl.ds(r0, tq), :], wq)
    q = (q * (HEAD_DIM ** -0.5)).astype(BF16)
    for h in range(n_q):
        kv = h // grp
        qh = q[:, h * HEAD_DIM:(h + 1) * HEAD_DIM]
        s = _nt(qh, kbuf[:, kv * HEAD_DIM:(kv + 1) * HEAD_DIM])
        m = jnp.max(s, axis=-1, keepdims=True)
        p = jnp.exp(s - m)
        l = jnp.sum(p, axis=-1, keepdims=True)
        o = _dot(p.astype(BF16), vbuf[:, kv * HEAD_DIM:(kv + 1) * HEAD_DIM])
        o_ref[:, h * HEAD_DIM:(h + 1) * HEAD_DIM] = o / l


def _attention(q, k, v, *, nb, seq, n_q, n_kv, tq, layer=None, kc=None, vc=None, gq=None, gk=None,
               cos=None, sin=None, k_cache=None):
    wq, wk = n_q * HEAD_DIM, n_kv * HEAD_DIM
    nq = seq // tq
    l_ctx = 0 if kc is None else kc.shape[2]
    norm, rope = gq is not None, cos is not None
    emit_k = k_cache is not None
    stacked = pl.BlockSpec((None, None, seq, wk), lambda b, j: (b, layer, 0, 0))
    flat = pl.BlockSpec((seq, wk), lambda b, j: (b, 0))
    args = [q, k, v]
    in_specs = [pl.BlockSpec((tq, wq), lambda b, j: (b * nq + j, 0)),
                stacked if k.ndim == 4 else flat, stacked if v.ndim == 4 else flat]
    if l_ctx:
        args += [kc, vc]
        in_specs += [pl.BlockSpec((None, None, l_ctx, wk), lambda b, j: (b, layer, 0, 0))] * 2
    if norm:
        args += [gq, gk]
        in_specs += [pl.BlockSpec((1, wq), lambda b, j: (0, 0)), pl.BlockSpec((1, wk), lambda b, j: (0, 0))]
    if rope:
        args += [cos, sin]
        in_specs += [pl.BlockSpec((seq, LANES), lambda b, j: (0, 0))] * 2
    out_shape = [jax.ShapeDtypeStruct((nb * seq, wq), F32)]
    out_specs = [pl.BlockSpec((tq, wq), lambda b, j: (b * nq + j, 0))]
    aliases = {}
    if emit_k:
        aliases[len(args)] = 1
        args.append(k_cache)
        in_specs.append(pl.BlockSpec(memory_space=pl.ANY))
        out_shape.append(jax.ShapeDtypeStruct(k_cache.shape, F32))
        out_specs.append(stacked)
    kern = functools.partial(_attn_kernel, n_q=n_q, n_kv=n_kv, tq=tq, l_self=seq, l_ctx=l_ctx,
                             norm=norm, rope=rope, emit_k=emit_k)
    return pl.pallas_call(
        kern,
        out_shape=tuple(out_shape),
        grid=(nb, nq),
        in_specs=in_specs,
        out_specs=tuple(out_specs),
        input_output_aliases=aliases,
        scratch_shapes=[pltpu.VMEM((seq + l_ctx, wk), BF16), pltpu.VMEM((seq + l_ctx, wk), BF16)],
        compiler_params=_cparams(("arbitrary", "arbitrary")),
        name="attention",
    )(*args)


def _nbr_plan(rows):
    nblk = rows // NBR_G
    kstart, pairs, ent = [], {}, []
    for blk in range(nblk):
        r0 = blk * NBR_G
        ks = min(max(r0 - WIN_R // 2, 0), rows - NBR_KROWS)
        kstart.append(ks)
        for i in range(NBR_G):
            r = r0 + i
            rs = min(max(r - WIN_R // 2, 0), rows - WIN_R)
            assert ks <= rs and rs + WIN_R <= ks + NBR_KROWS
            for p in range(NBR_KROWS // 2):
                ab = []
                for krow in (ks + 2 * p, ks + 2 * p + 1):
                    valid = rs <= krow < rs + WIN_R
                    ab.append(krow - r + WIN_R if valid else 0)
                ent.append(pairs.setdefault(tuple(ab), len(pairs)))
    pair_list = sorted(pairs, key=pairs.get)
    return (np.asarray(kstart, np.int32), np.asarray(ent, np.int32),
            np.asarray([a for a, _ in pair_list], np.int32), np.asarray([b for _, b in pair_list], np.int32))


def _nbr_bias_tables(rpb, rows):
    _, _, left, right = _nbr_plan(rows)
    col = np.arange(GRID_W)
    c_start = np.clip(col - WIN_C // 2, 0, GRID_W - WIN_C)
    cmask = (col[None, :] >= c_start[:, None]) & (col[None, :] < c_start[:, None] + WIN_C)
    dc = np.clip(col[None, :] - col[:, None] + (WIN_C - 1), 0, 2 * WIN_C - 2)
    onehot = (dc.reshape(-1)[None, :] == np.arange(2 * WIN_C - 1)[:, None]).astype(np.float32)
    t = jnp.einsum("lhde,ek->lhdk", rpb.astype(F32), jnp.asarray(onehot), precision=lax.Precision.HIGHEST)
    t = t.reshape(rpb.shape[:3] + (GRID_W, GRID_W))
    t = jnp.where(jnp.asarray(cmask), t, NEG_INF)
    masked = jnp.full(t.shape[:2] + (1, GRID_W, GRID_W), NEG_INF, F32)
    tpad = jnp.concatenate([masked, t], axis=2)
    pick = lambda idx: jnp.concatenate([tpad[:, :, int(a):int(a) + 1] for a in idx], axis=2)
    return jnp.concatenate([pick(left), pick(right)], axis=-1)


def _nbr_kernel(ks_ref, ent_ref, q_ref, k_ref, v_ref, kc_ref, vc_ref, tt_ref, o_ref,
                kbuf, vbuf, kcbuf, vcbuf, s_ref, *, seq):
    blk = pl.program_id(1)
    nk = NBR_KROWS * GRID_W
    npair = NBR_KROWS // 2
    rb = 256

    @pl.when(blk == 0)
    def _():
        def body(i, carry):
            r0 = pl.multiple_of(i * rb, rb)
            kbuf[pl.ds(r0, rb), :] = k_ref[pl.ds(r0, rb), :].astype(BF16)
            vbuf[pl.ds(r0, rb), :] = v_ref[pl.ds(r0, rb), :].astype(BF16)
            return carry
        lax.fori_loop(0, seq // rb, body, 0)
        kcbuf[...] = kc_ref[...].astype(BF16)
        vcbuf[...] = vc_ref[...].astype(BF16)

    ks = pl.multiple_of(ks_ref[blk] * GRID_W, GRID_W)
    q = (q_ref[...] * (HEAD_DIM ** -0.5)).astype(BF16)
    for h in range(C_HEADS):
        hs = slice(h * HEAD_DIM, (h + 1) * HEAD_DIM)
        qh = q[:, hs]
        s_ref[...] = _nt(qh, kbuf[pl.ds(ks, nk), hs])
        for i in range(NBR_G):
            for p in range(npair):
                e = ent_ref[(blk * NBR_G + i) * npair + p]
                rs_, cs_ = slice(i * GRID_W, (i + 1) * GRID_W), slice(p * LANES, (p + 1) * LANES)
                s_ref[rs_, cs_] = s_ref[rs_, cs_] + tt_ref[h, e]
        s_loc = s_ref[...]
        s_ctx = _nt(qh, kcbuf[:, hs])
        m = jnp.maximum(jnp.max(s_loc, axis=-1, keepdims=True), jnp.max(s_ctx, axis=-1, keepdims=True))
        p_loc = jnp.exp(s_loc - m)
        p_ctx = jnp.exp(s_ctx - m)
        l = jnp.sum(p_loc, axis=-1, keepdims=True) + jnp.sum(p_ctx, axis=-1, keepdims=True)
        o = _dot(p_loc.astype(BF16), vbuf[pl.ds(ks, nk), hs]) + _dot(p_ctx.astype(BF16), vcbuf[:, hs])
        o_ref[:, hs] = o / l


def _nbr_attention(q, k, v, kc, vc, tt, *, nb, seq, layer):
    rows = seq // GRID_W
    kstart, ent, _, _ = _nbr_plan(rows)
    nblk = rows // NBR_G
    tq = NBR_G * GRID_W
    nk = NBR_KROWS * GRID_W
    l_ctx = kc.shape[2]
    n_ent = tt.shape[2]
    smem = pl.BlockSpec(memory_space=pltpu.SMEM)
    return pl.pallas_call(
        functools.partial(_nbr_kernel, seq=seq),
        out_shape=jax.ShapeDtypeStruct((nb * seq, C_WIDTH), F32),
        grid=(nb, nblk),
        in_specs=[smem, smem,
                  pl.BlockSpec((tq, C_WIDTH), lambda b, j: (b * nblk + j, 0)),
                  pl.BlockSpec((seq, C_WIDTH), lambda b, j: (b, 0)),
                  pl.BlockSpec((seq, C_WIDTH), lambda b, j: (b, 0)),
                  pl.BlockSpec((None, None, l_ctx, C_WIDTH), lambda b, j: (b, layer, 0, 0)),
                  pl.BlockSpec((None, None, l_ctx, C_WIDTH), lambda b, j: (b, layer, 0, 0)),
                  pl.BlockSpec((None, C_HEADS, n_ent, GRID_W, LANES), lambda b, j: (layer, 0, 0, 0, 0))],
        out_specs=pl.BlockSpec((tq, C_WIDTH), lambda b, j: (b * nblk + j, 0)),
        scratch_shapes=[pltpu.VMEM((seq, C_WIDTH), BF16), pltpu.VMEM((seq, C_WIDTH), BF16),
                        pltpu.VMEM((l_ctx, C_WIDTH), BF16), pltpu.VMEM((l_ctx, C_WIDTH), BF16),
                        pltpu.VMEM((tq, nk), F32)],
        compiler_params=_cparams(("arbitrary", "arbitrary")),
        name="nbr_attention",
    )(jnp.asarray(kstart), jnp.asarray(ent), q, k, v, kc, vc, tt)


def _bd(x, mask):
    return jnp.concatenate([x] * B_HEADS, axis=0) * mask[jnp.dtype(x.dtype).name]


def _unpack_bd(full, lane_masks):
    out = None
    for g in range(B_HEADS):
        blk = full[g * HEAD_DIM:(g + 1) * HEAD_DIM, :] * lane_masks[g]
        out = blk if out is None else out + blk
    return out


def _pk_mm(a, b, mask):
    return _dot(a.astype(BF16), _bd(b.astype(BF16), mask))


def _pk_unit_tri_inverses(lmats, lowers, ri, cj, mask):
    eye = jnp.where(ri == cj, 1.0, 0.0).astype(F32)
    ts = [eye] * len(lmats)
    s = 1
    while s < CHUNK:
        same_pair = ((ri ^ cj) & ~(2 * s - 1)) == 0
        r_hi, c_hi = (ri & s) != 0, (cj & s) != 0
        sel = {True: jnp.where(same_pair & r_hi & ~c_hi, 1.0, 0.0), False: jnp.where(same_pair & ~r_hi & c_hi, 1.0, 0.0)}
        offs = [lm * sel[lo] for lm, lo in zip(lmats, lowers)]
        if s == 1:
            ts = [t - off for t, off in zip(ts, offs)]
        else:
            mids = [_pk_mm(t, off, mask) for t, off in zip(ts, offs)]
            ts = [t - _pk_mm(m, t, mask) for t, m in zip(ts, mids)]
        s *= 2
    return ts


def _delta_kernel(*refs, seq, has_s0, has_cache):
    it = iter(refs)
    zb_ref = next(it)
    s0_ref = next(it) if has_s0 else None
    convw_ref, alog_ref, dtb_ref, gon_ref = next(it), next(it), next(it), next(it)
    if has_cache:
        next(it)
    o_ref, sfin_ref = next(it), next(it)
    qt_s, au_s, p_s, n_s, gam_s, of_s, ob_s, st_s = (next(it) for _ in range(8))

    n = seq // CHUNK
    c64 = CHUNK
    pw = B_WIDTH
    row = lax.broadcasted_iota(jnp.int32, (c64, 1), 0)
    lane = lax.broadcasted_iota(jnp.int32, (c64, LANES), 1)
    ri = lax.broadcasted_iota(jnp.int32, (c64, pw), 0)
    lane_p = lax.broadcasted_iota(jnp.int32, (c64, pw), 1)
    cj = lane_p & (HEAD_DIM - 1)
    lane_head = [jnp.where(lane_p // HEAD_DIM == g, 1.0, 0.0) for g in range(B_HEADS)]
    off_diag = jnp.where(ri == cj, 0.0, 1.0)
    bd_bool = (lax.broadcasted_iota(jnp.int32, (pw, pw), 0) // HEAD_DIM
               == lax.broadcasted_iota(jnp.int32, (pw, pw), 1) // HEAD_DIM)
    bd_mask = {"float32": jnp.where(bd_bool, 1.0, 0.0), "bfloat16": jnp.where(bd_bool, 1.0, 0.0).astype(BF16)}
    ones_b = _head_ones(pw)
    w0, w1, w2 = convw_ref[0:1, :], convw_ref[1:2, :], convw_ref[2:3, :]
    neg_decay_rate = -jnp.exp(alog_ref[...])
    eye8 = jnp.where(lax.broadcasted_iota(jnp.int32, (8, LANES), 0) == lax.broadcasted_iota(jnp.int32, (8, LANES), 1),
                     1.0, 0.0).astype(BF16)
    expand = jnp.where(lax.broadcasted_iota(jnp.int32, (LANES, N_DIR * pw), 1) // HEAD_DIM
                       == lax.broadcasted_iota(jnp.int32, (LANES, N_DIR * pw), 0), 1.0, 0.0).astype(BF16)

    def chunk_terms(c):
        r0 = pl.multiple_of(c * c64, c64)
        x = zb_ref[pl.ds(r0, c64), 0:3 * B_WIDTH]
        prev = zb_ref[pl.ds(jnp.maximum(r0 - 1, 0), 1), 0:3 * B_WIDTH] * jnp.where(c > 0, 1.0, 0.0)
        nxt = zb_ref[pl.ds(jnp.minimum(r0 + c64, seq - 1), 1), 0:3 * B_WIDTH] * jnp.where(c < n - 1, 1.0, 0.0)
        x_up = jnp.where(row == 0, prev, pltpu.roll(x, 1, 0))
        x_dn = jnp.where(row == c64 - 1, nxt, pltpu.roll(x, c64 - 1, 0))
        y = _silu(x_up * w0 + x * w1 + x_dn * w2)
        q, k, v = y[:, 0:pw], y[:, pw:2 * pw], y[:, 2 * pw:]
        q = q * lax.rsqrt(_head_sum(q * q, ones_b) + EPS) * (HEAD_DIM ** -0.5)
        k = k * lax.rsqrt(_head_sum(k * k, ones_b) + EPS)

        ga = zb_ref[pl.ds(r0, c64), GATE_OFF:GATE_OFF + LANES]
        beta = jax.nn.sigmoid(ga)
        a = pltpu.roll(ga, LANES - N_DIR * B_HEADS, 1) + dtb_ref[...]
        g = neg_decay_rate * (jnp.maximum(a, 0.0) + jnp.log(1.0 + jnp.exp(-jnp.abs(a))))
        cf, cr = g, g
        sh = 1
        while sh < c64:
            cf = cf + jnp.where(row >= sh, pltpu.roll(cf, sh, 0), 0.0)
            cr = cr + jnp.where(row < c64 - sh, pltpu.roll(cr, c64 - sh, 0), 0.0)
            sh *= 2
        gc = jnp.where(lane < B_HEADS, cf, cr)
        g3 = _split3(gc)
        gct = _nt(eye8, g3[0]) + (_nt(eye8, g3[1]) + _nt(eye8, g3[2]))
        b2 = _split2(beta)
        ex = _dot(jnp.concatenate([b2[0], b2[1], g3[0], g3[1], g3[2]], axis=0), expand)
        bexp = ex[0:c64] + ex[c64:2 * c64]
        gexp = ex[2 * c64:3 * c64] + (ex[3 * c64:4 * c64] + ex[4 * c64:])

        k_bd = _bd(k.astype(BF16), bd_mask)
        q_bf = q.astype(BF16)
        probs = []
        for d in range(N_DIR):
            ls = slice(d * pw, (d + 1) * pw)
            bx, gx = bexp[:, ls], gexp[:, ls]
            incl = (ri >= cj) if d == 0 else (ri <= cj)
            last = c64 - 1 if d == 0 else 0
            g_row = jnp.concatenate([gct[d * B_HEADS + h:d * B_HEADS + h + 1, :] for h in range(B_HEADS)], axis=1)
            g_last = gx[last:last + 1, :]
            decay = jnp.exp(jnp.where(incl, gx - g_row, NEG_INF))
            kb = k * bx
            aa = _nt(jnp.concatenate([kb.astype(BF16), q_bf], axis=0), k_bd)
            eg = jnp.exp(gx)
            probs.append(dict(
                c=c, d=d, lmat=aa[0:c64] * (decay * off_diag),
                a_intra=(aa[c64:] * decay).astype(BF16),
                vb=v * bx, kbe=kb * eg, qe=q * eg, k_dec=(k * jnp.exp(g_last - gx)).astype(BF16),
                gam=jnp.exp(g_last)))
        return probs

    def terms_loop(i, carry):
        probs = [p for j in range(DELTA_UNROLL) for p in chunk_terms(i * DELTA_UNROLL + j)]
        tinvs = _pk_unit_tri_inverses([p["lmat"] for p in probs], [p["d"] == 0 for p in probs], ri, cj, bd_mask)
        us = [_pk_mm(t, p["vb"], bd_mask).astype(BF16) for t, p in zip(tinvs, probs)]
        ws = [_pk_mm(t, p["kbe"], bd_mask).astype(BF16) for t, p in zip(tinvs, probs)]
        for p, u, w in zip(probs, us, ws):
            c, d = p["c"], p["d"]
            qt_s[c, d] = (p["qe"] - _dot(p["a_intra"], _bd(w, bd_mask))).astype(BF16)
            au_s[c, d] = _dot(p["a_intra"], _bd(u, bd_mask))
            full = _tn(p["k_dec"], jnp.concatenate([w, u], axis=1))
            p_s[c, d] = _unpack_bd(full[:, 0:pw], lane_head).astype(BF16)
            n_s[c, d] = _unpack_bd(full[:, pw:], lane_head)
            gam_s[c, d] = jnp.broadcast_to(p["gam"], (8, pw))
        return carry

    lax.fori_loop(0, n // DELTA_UNROLL, terms_loop, 0)

    for d in range(N_DIR):
        if has_s0:
            st_s[d] = jnp.concatenate([s0_ref[d, h] for h in range(B_HEADS)], axis=1)
        else:
            st_s[d] = jnp.zeros((c64, pw), F32)

    def scan(t, carry):
        for d in range(N_DIR):
            c = t if d == 0 else n - 1 - t
            r0 = pl.multiple_of(c * c64, c64)
            s = st_s[d]
            both = _dot(jnp.concatenate([qt_s[c, d], p_s[c, d]], axis=0), _bd(s.astype(BF16), bd_mask))
            out_s = of_s if d == 0 else ob_s
            out_s[pl.ds(r0, c64), :] = both[0:c64] + au_s[c, d]
            st_s[d] = (s * gam_s[c, d, 0:1, :] - both[c64:]) + n_s[c, d]
        return carry

    lax.fori_loop(0, n, scan, 0)

    fb = DELTA_UNROLL * c64

    def finish(i, carry):
        r0 = pl.multiple_of(i * fb, fb)
        o = of_s[pl.ds(r0, fb), :] + ob_s[pl.ds(r0, fb), :]
        y = _head_rmsnorm(o, gon_ref[...], ones_b)
        o_ref[pl.ds(r0, fb), :] = y * _silu(zb_ref[pl.ds(r0, fb), 3 * B_WIDTH:4 * B_WIDTH])
        return carry

    lax.fori_loop(0, n // DELTA_UNROLL, finish, 0)
    for d in range(N_DIR):
        s_pk = st_s[d]
        for h in range(B_HEADS):
            sfin_ref[d, h] = s_pk[:, h * HEAD_DIM:(h + 1) * HEAD_DIM]


def _deltanet(zb, s0, conv_w, a_log, dt_bias, g_onorm, *, nb, seq, layer=None, s_cache=None):
    n = seq // CHUNK
    has_s0 = s0 is not None
    pad = LANES - N_DIR * B_HEADS
    alog_row = jnp.pad(a_log.reshape(1, -1).astype(F32), ((0, 0), (0, pad)))
    dtb_row = jnp.pad(dt_bias.reshape(1, -1).astype(F32), ((0, 0), (0, pad)))
    gon_row = jnp.tile(g_onorm.astype(F32), B_HEADS).reshape(1, B_WIDTH)
    st_shape = (N_DIR, B_HEADS, HEAD_DIM, HEAD_DIM)
    args = [zb]
    in_specs = [pl.BlockSpec((seq, ZB_W), lambda b: (b, 0))]
    if has_s0:
        args.append(s0)
        in_specs.append(pl.BlockSpec((None, None) + st_shape, lambda b: (b, layer, 0, 0, 0, 0)))
    args += [conv_w.astype(F32), alog_row, dtb_row, gon_row]
    in_specs += [pl.BlockSpec((3, 3 * B_WIDTH), lambda b: (0, 0)),
                 pl.BlockSpec((1, LANES), lambda b: (0, 0)),
                 pl.BlockSpec((1, LANES), lambda b: (0, 0)),
                 pl.BlockSpec((1, B_WIDTH), lambda b: (0, 0))]
    per_chunk = (n, N_DIR, CHUNK, B_WIDTH)
    aliases = {}
    if s_cache is None:
        st_out = jax.ShapeDtypeStruct((nb,) + st_shape, F32)
        st_spec = pl.BlockSpec((None,) + st_shape, lambda b: (b, 0, 0, 0, 0))
    else:
        aliases[len(args)] = 1
        args.append(s_cache)
        in_specs.append(pl.BlockSpec(memory_space=pl.ANY))
        st_out = jax.ShapeDtypeStruct(s_cache.shape, F32)
        st_spec = pl.BlockSpec((None, None) + st_shape, lambda b: (b, layer, 0, 0, 0, 0))
    return pl.pallas_call(
        functools.partial(_delta_kernel, seq=seq, has_s0=has_s0, has_cache=s_cache is not None),
        out_shape=(jax.ShapeDtypeStruct((nb * seq, B_WIDTH), F32), st_out),
        grid=(nb,),
        in_specs=in_specs,
        out_specs=(pl.BlockSpec((seq, B_WIDTH), lambda b: (b, 0)), st_spec),
        input_output_aliases=aliases,
        scratch_shapes=[pltpu.VMEM(per_chunk, BF16), pltpu.VMEM(per_chunk, F32),
                        pltpu.VMEM(per_chunk, BF16), pltpu.VMEM(per_chunk, F32),
                        pltpu.VMEM((n, N_DIR, 8, B_WIDTH), F32),
                        pltpu.VMEM((seq, B_WIDTH), F32), pltpu.VMEM((seq, B_WIDTH), F32),
                        pltpu.VMEM((N_DIR, CHUNK, B_WIDTH), F32)],
        compiler_params=_cparams(("arbitrary",)),
        name="deltanet",
    )(*args)


def _post_kernel(x_ref, oa_ref, ob_ref, oc_ref, mod_ref, gn_ref, ga_ref, gc_ref,
                 wo_ref, wgu_ref, wd_ref, y_ref):
    x = x_ref[...]
    ya = _rmsnorm(oa_ref[...], ga_ref[...])
    yc = _rmsnorm(oc_ref[...], gc_ref[...])
    y = jnp.concatenate([ya, ob_ref[...], yc], axis=-1).astype(BF16)
    m = _dot(y, wo_ref[...])
    x1 = x + mod_ref[0, 2:3, :] * _rmsnorm(m, gn_ref[1:2, :])
    h = _rmsnorm(x1, gn_ref[2:3, :]) * (1.0 + mod_ref[0, 4:5, :]) + mod_ref[0, 3:4, :]
    gu = _dot(h.astype(BF16), wgu_ref[...])
    act = (_silu(gu[:, 0:D_FF]) * gu[:, D_FF:]).astype(BF16)
    f = _dot(act, wd_ref[...])
    y_ref[...] = x1 + mod_ref[0, 5:6, :] * _rmsnorm(f, gn_ref[3:4, :])


def _post(x, oa, ob, oc, mod, g_norm, g_out_a, g_out_c, w_out, w_gu, w_down, layer):
    m, d = x.shape
    tm = TOKEN_TILE
    nb = mod.shape[0]
    per = (m // nb) // tm
    row = lambda i: (i, 0)
    const = lambda i: (0, 0)
    wspec = lambda w: pl.BlockSpec((None,) + w.shape[1:], lambda i: (layer, 0, 0), pipeline_mode=pl.Buffered(1))
    return pl.pallas_call(
        _post_kernel,
        out_shape=jax.ShapeDtypeStruct((m, d), F32),
        grid=(m // tm,),
        in_specs=[pl.BlockSpec((tm, d), row),
                  pl.BlockSpec((tm, A_WIDTH), row),
                  pl.BlockSpec((tm, B_WIDTH), row),
                  pl.BlockSpec((tm, C_WIDTH), row),
                  pl.BlockSpec((1, N_MOD, d), lambda i: (i // per, 0, 0)),
                  pl.BlockSpec((4, d), const),
                  pl.BlockSpec((1, A_WIDTH), const),
                  pl.BlockSpec((1, C_WIDTH), const),
                  wspec(w_out), wspec(w_gu), wspec(w_down)],
        out_specs=pl.BlockSpec((tm, d), row),
        compiler_params=_cparams(("arbitrary",)),
        name="post",
    )(x, oa, ob, oc, mod, g_norm, g_out_a, g_out_c, w_out, w_gu, w_down)


def _rope_tables(seq):
    quarter = HEAD_DIM // 4
    inv = ROPE_THETA ** (-jnp.arange(quarter, dtype=F32) / quarter)
    t = jnp.arange(seq)
    rowp = (t // GRID_W).astype(F32)
    colp = (t % GRID_W).astype(F32)
    ang = jnp.concatenate([rowp[:, None] * inv, colp[:, None] * inv], axis=-1)
    cos, sin = jnp.cos(ang), jnp.sin(ang)
    reps = LANES // HEAD_DIM
    return (jnp.tile(jnp.concatenate([cos, cos], axis=-1), (1, reps)),
            jnp.tile(jnp.concatenate([-sin, sin], axis=-1), (1, reps)))


def _split_w_in(w_in):
    n_gate = 2 * N_DIR * B_HEADS
    cut = ZA_W + 4 * B_WIDTH
    w_g = jnp.pad(w_in[..., cut:cut + n_gate], ((0, 0), (0, 0), (0, LANES - n_gate)))
    return w_in[..., :cut].astype(BF16), w_g.astype(BF16), w_in[..., cut + n_gate:].astype(BF16)


def kernel(x_prompt, x_sample, cache_a_k, cache_a_v, state_b, cache_c_k, cache_c_v, c, c_ctx, w_mod, b_mod,
           g_norm, w_in, g_qk_a, g_out_a, conv_w, a_log, dt_bias, g_onorm_b, rpb, g_out_c, w_out, w_gu, w_down):
    nbp, seq_p, d = x_prompt.shape
    nbs, seq_s, _ = x_sample.shape
    depth = w_mod.shape[0]
    past = cache_a_k.shape[2]

    cvec = jnp.concatenate([c_ctx[None, :], c, jnp.zeros((8 - 1 - nbs, d), F32)], axis=0)
    mods = _modulation(cvec, w_mod, b_mod)
    w_ab, w_g, w_c = _split_w_in(w_in)
    w_out_b, w_gu_b, w_down_b = w_out.astype(BF16), w_gu.astype(BF16), w_down.astype(BF16)
    cos, sin = _rope_tables(seq_s)
    tt = _nbr_bias_tables(rpb, seq_s // GRID_W)
    ck_a = cache_a_k.reshape(nbs, depth, past, A_KV_WIDTH)
    cv_a = cache_a_v.reshape(nbs, depth, past, A_KV_WIDTH)
    ck_c = cache_c_k.reshape(nbs, depth, past, C_WIDTH)
    cv_c = cache_c_v.reshape(nbs, depth, past, C_WIDTH)

    xp = x_prompt.reshape(nbp * seq_p, d)
    xs = x_sample.reshape(nbs * seq_s, d)
    ka_st = jnp.zeros((nbp, depth, seq_p, A_KV_WIDTH), F32)
    va_st = jnp.zeros((nbp, depth, seq_p, A_KV_WIDTH), F32)
    kc_st = jnp.zeros((nbp, depth, seq_p, C_WIDTH), F32)
    vc_st = jnp.zeros((nbp, depth, seq_p, C_WIDTH), F32)
    sb_st = jnp.zeros((nbp, depth, N_DIR, B_HEADS, HEAD_DIM, HEAD_DIM), F32)
    for l in range(depth):
        gq = jnp.tile(g_qk_a[l, 0], A_HEADS).reshape(1, A_WIDTH)
        gk = jnp.tile(g_qk_a[l, 1], A_KV_HEADS).reshape(1, A_KV_WIDTH)
        goa, goc = g_out_a[l].reshape(1, A_WIDTH), g_out_c[l].reshape(1, C_WIDTH)
        g0 = g_norm[l, 0:1]
        mod_p = mods[l, 0:1].reshape(1, N_MOD, d)
        mod_s = mods[l, 1:1 + nbs].reshape(nbs, N_MOD, d)

        aq, ak, va_st, zb, cq, kc_st, vc_st = _inproj(xp, mod_p, g0, w_ab, w_g, w_c, l, caches=(va_st, kc_st, vc_st))
        o_a, ka_st = _attention(aq, ak, va_st, nb=nbp, seq=seq_p, n_q=A_HEADS, n_kv=A_KV_HEADS, tq=seq_p, layer=l,
                                gq=gq, gk=gk, k_cache=ka_st)
        o_b, sb_st = _deltanet(zb, None, conv_w[l], a_log[l], dt_bias[l], g_onorm_b[l], nb=nbp, seq=seq_p,
                               layer=l, s_cache=sb_st)
        (o_c,) = _attention(cq, kc_st, vc_st, nb=nbp, seq=seq_p, n_q=C_HEADS, n_kv=C_HEADS, tq=seq_p, layer=l)
        xp = _post(xp, o_a, o_b, o_c, mod_p, g_norm[l], goa, goc, w_out_b, w_gu_b, w_down_b, l)

        aq, ak, av, zb, cq, ck, cv = _inproj(xs, mod_s, g0, w_ab, w_g, w_c, l)
        (o_a,) = _attention(aq, ak, av, nb=nbs, seq=seq_s, n_q=A_HEADS, n_kv=A_KV_HEADS, tq=Q_TILE, layer=l,
                            kc=ck_a, vc=cv_a, gq=gq, gk=gk, cos=cos, sin=sin)
        o_b, _ = _deltanet(zb, state_b, conv_w[l], a_log[l], dt_bias[l], g_onorm_b[l], nb=nbs, seq=seq_s, layer=l)
        o_c = _nbr_attention(cq, ck, cv, ck_c, cv_c, tt, nb=nbs, seq=seq_s, layer=l)
        xs = _post(xs, o_a, o_b, o_c, mod_s, g_norm[l], goa, goc, w_out_b, w_gu_b, w_down_b, l)

    heads = lambda st, n_heads: st.reshape(nbp, depth, seq_p, n_heads, HEAD_DIM)
    return (xp.reshape(nbp, seq_p, d), xs.reshape(nbs, seq_s, d),
            heads(ka_st, A_KV_HEADS), heads(va_st, A_KV_HEADS), sb_st, heads(kc_st, C_HEADS), heads(vc_st, C_HEADS))
```
